```python
import math
import jax, jax.numpy as jnp
from jax import lax
import numpy as np

D_MODEL = 2048
BATCH = 16
SEQ = 2048
DEPTH = 1
DEC_BATCH = 32
DEC_SEQ = 16
PAST_LEN = 2048

CHUNK = 64
N_PAST_CHUNKS = 8
BAND_PAST = CHUNK * N_PAST_CHUNKS
ATT_WIDTH = D_MODEL // 2
HEAD_DIM_ATT = 128
N_HEADS_ATT = ATT_WIDTH // HEAD_DIM_ATT
MAX_REL = 128
GLA_V_WIDTH = D_MODEL - ATT_WIDTH
N_HEADS_GLA = 4
DV_GLA = GLA_V_WIDTH // N_HEADS_GLA
DK_GLA = DV_GLA // 2
GLA_QK_WIDTH = N_HEADS_GLA * DK_GLA
GATE_RANK = 16
GATE_TAU = 16.0
GLA_BLOCK = 16
IN_SIZES = (ATT_WIDTH, ATT_WIDTH, ATT_WIDTH, GLA_QK_WIDTH, GLA_QK_WIDTH, GLA_V_WIDTH, GLA_V_WIDTH, GATE_RANK)
IN_WIDTH = 3 * ATT_WIDTH + 2 * GLA_QK_WIDTH + 2 * GLA_V_WIDTH + GATE_RANK
MIX_WIDTH = ATT_WIDTH + GLA_V_WIDTH
D_FF = 5632
CONV_W = 3
EPS = 1e-6

kernel_name = 'hybrid_streaming_encoder_step'


def rms_norm(x, g):
    xf = x.astype(jnp.float32)
    y = xf * lax.rsqrt(jnp.mean(xf * xf, axis=-1, keepdims=True) + EPS)
    return (y * g.astype(jnp.float32)).astype(x.dtype)


def split_in_proj(p):
    offs = []
    acc = 0
    for s in IN_SIZES[:-1]:
        acc += s
        offs.append(acc)
    return jnp.split(p, offs, axis=-1)


def band_attention(q, k, v, q_pos, k_pos, rel_bias):
    rel = jnp.clip(q_pos[:, :, None] - k_pos[:, None, :], -MAX_REL, MAX_REL) + MAX_REL
    bias = jnp.transpose(rel_bias[:, rel], (1, 0, 2, 3)).astype(jnp.float32)
    s = jnp.einsum('bnqhd,bnkhd->bnhqk', q.astype(jnp.float32), k.astype(jnp.float32)) * (HEAD_DIM_ATT ** -0.5) + bias
    s = jnp.where((k_pos >= 0)[None, :, None, None, :], s, -1e30)
    p = jax.nn.softmax(s, axis=-1)
    o = jnp.einsum('bnhqk,bnkhd->bnqhd', p, v.astype(jnp.float32))
    return o.astype(q.dtype)


def prompt_attention(q, k, v, rel_bias):
    B, T, H, Dh = q.shape
    nc = T // CHUNK
    nb = N_PAST_CHUNKS + 1
    band_idx = jnp.arange(nc)[:, None] + jnp.arange(nb)[None, :]

    def gather_band(t):
        tc = jnp.pad(t.reshape(B, nc, CHUNK, H, Dh), ((0, 0), (N_PAST_CHUNKS, 0), (0, 0), (0, 0), (0, 0)))
        return tc[:, band_idx].reshape(B, nc, nb * CHUNK, H, Dh)

    q_pos = jnp.arange(T).reshape(nc, CHUNK)
    k_pos = ((band_idx - N_PAST_CHUNKS)[:, :, None] * CHUNK + jnp.arange(CHUNK)[None, None, :]).reshape(nc, nb * CHUNK)
    o = band_attention(q.reshape(B, nc, CHUNK, H, Dh), gather_band(k), gather_band(v), q_pos, k_pos, rel_bias)
    return o.reshape(B, T, H, Dh)


def sample_attention(q, k, v, cache_k, cache_v, rel_bias):
    B, T, H, Dh = q.shape
    L = cache_k.shape[1]
    kb = jnp.concatenate([cache_k.astype(k.dtype), k], axis=1)[:, None]
    vb = jnp.concatenate([cache_v.astype(v.dtype), v], axis=1)[:, None]
    q_pos = (PAST_LEN + jnp.arange(T))[None]
    k_pos = (PAST_LEN - L + jnp.arange(L + T))[None]
    o = band_attention(q[:, None], kb, vb, q_pos, k_pos, rel_bias)
    return o[:, 0]


def gla(q, k, v, log_a, state0):
    B, T, H, Dk = q.shape
    Dv = v.shape[-1]
    nblk = -(-T // GLA_BLOCK)
    pad = nblk * GLA_BLOCK - T

    def blocks(t):
        t = jnp.pad(t.astype(jnp.float32), ((0, 0), (0, pad), (0, 0), (0, 0)))
        return t.reshape(B, nblk, GLA_BLOCK, H, -1).transpose(1, 0, 3, 2, 4)

    qb, kb, vb = blocks(q), blocks(k), blocks(v)
    cum = jnp.cumsum(blocks(log_a), axis=3)
    causal = jnp.tril(jnp.ones((GLA_BLOCK, GLA_BLOCK), dtype=bool))

    def step(s, blk):
        qi, ki, vi, ci = blk
        c_last = ci[:, :, -1:, :]
        q_t = qi * jnp.exp(ci)
        k_t = ki * jnp.exp(-ci)
        a = jnp.where(causal, jnp.einsum('bhqd,bhkd->bhqk', q_t, k_t), 0.0)
        o = jnp.einsum('bhqk,bhkv->bhqv', a, vi) + jnp.einsum('bhqd,bhdv->bhqv', q_t, s)
        s = jnp.exp(c_last[:, :, 0, :, None]) * s + jnp.einsum('bhkd,bhkv->bhdv', ki * jnp.exp(c_last - ci), vi)
        return s, o

    s_final, o = lax.scan(step, state0.astype(jnp.float32), (qb, kb, vb, cum))
    o = o.transpose(1, 0, 3, 2, 4).reshape(B, nblk * GLA_BLOCK, H, Dv)[:, :T]
    return o, s_final


def conv_ffn(x, conv_prev, g_pre, w_up, w_conv, b_conv, w_down, g_post):
    T = x.shape[1]
    h = rms_norm(x, g_pre)
    gate, val = jnp.split(h @ w_up, 2, axis=-1)
    ext = jnp.concatenate([conv_prev.astype(gate.dtype), gate], axis=1)
    conv = b_conv + sum(ext[:, i:i + T] * w_conv[i] for i in range(CONV_W))
    y = (jax.nn.silu(conv) * val) @ w_down
    return rms_norm(y, g_post), ext[:, T:]


def encoder_layer(x, att_fn, gla_state0, conv_prev, g_mix_pre, w_in, w_gate_up, b_gate, g_gla, w_o,
                  g_mix_post, g_ffn_pre, w_up, w_conv, b_conv, w_down, g_ffn_post):
    B, T, _ = x.shape
    h = rms_norm(x, g_mix_pre)
    q_a, k_a, v_a, q_g, k_g, v_g, r_g, a_lo = split_in_proj(h @ w_in)
    heads = lambda t, n: t.reshape(B, T, n, -1)
    q_a, k_a, v_a = heads(q_a, N_HEADS_ATT), heads(k_a, N_HEADS_ATT), heads(v_a, N_HEADS_ATT)
    o_att = att_fn(q_a, k_a, v_a).reshape(B, T, ATT_WIDTH)
    log_a = jax.nn.log_sigmoid((a_lo @ w_gate_up + b_gate).astype(jnp.float32)) / GATE_TAU
    o_gla, s_gla = gla(heads(q_g, N_HEADS_GLA) * (DK_GLA ** -0.5), heads(k_g, N_HEADS_GLA),
                       heads(v_g, N_HEADS_GLA), heads(log_a, N_HEADS_GLA), gla_state0)
    o_gla = rms_norm(o_gla, g_gla).reshape(B, T, GLA_V_WIDTH).astype(x.dtype) * jax.nn.silu(r_g)
    mix = jnp.concatenate([o_att, o_gla], axis=-1) @ w_o
    x = x + rms_norm(mix, g_mix_post)
    f, conv_state = conv_ffn(x, conv_prev, g_ffn_pre, w_up, w_conv, b_conv, w_down, g_ffn_post)
    return x + f, k_a, v_a, s_gla, conv_state


def setup_inputs(seed: int = 0) -> dict:
    key = jax.random.key(seed)
    ks = jax.random.split(key, 24)
    nrm = lambda k, shape, scale: jax.random.normal(k, shape, jnp.float32) * scale
    gain = lambda k, n: 1.0 + 0.05 * jax.random.normal(k, (DEPTH, n), jnp.float32)
    att_cache = min(BAND_PAST, PAST_LEN)
    return {
        'x_prompt': nrm(ks[0], (BATCH, SEQ, D_MODEL), 1.0),
        'x_sample': nrm(ks[1], (DEC_BATCH, DEC_SEQ, D_MODEL), 1.0),
        'cache_k': nrm(ks[2], (DEPTH, DEC_BATCH, att_cache, N_HEADS_ATT, HEAD_DIM_ATT), 1.0),
        'cache_v': nrm(ks[3], (DEPTH, DEC_BATCH, att_cache, N_HEADS_ATT, HEAD_DIM_ATT), 1.0),
        'state_gla': nrm(ks[4], (DEPTH, DEC_BATCH, N_HEADS_GLA, DK_GLA, DV_GLA), 1.0),
        'state_conv': nrm(ks[5], (DEPTH, DEC_BATCH, CONV_W - 1, D_FF), 1.0),
        'g_mix_pre': gain(ks[6], D_MODEL),
        'w_in': nrm(ks[7], (DEPTH, D_MODEL, IN_WIDTH), D_MODEL ** -0.5),
        'w_gate_up': nrm(ks[8], (DEPTH, GATE_RANK, GLA_QK_WIDTH), GATE_RANK ** -0.5),
        'b_gate': nrm(ks[9], (DEPTH, GLA_QK_WIDTH), 0.1),
        'rel_bias': nrm(ks[10], (DEPTH, N_HEADS_ATT, 2 * MAX_REL + 1), 0.1),
        'g_gla': gain(ks[11], DV_GLA),
        'w_o': nrm(ks[12], (DEPTH, MIX_WIDTH, D_MODEL), MIX_WIDTH ** -0.5),
        'g_mix_post': gain(ks[13], D_MODEL),
        'g_ffn_pre': gain(ks[14], D_MODEL),
        'w_up': nrm(ks[15], (DEPTH, D_MODEL, 2 * D_FF), D_MODEL ** -0.5),
        'w_conv': nrm(ks[16], (DEPTH, CONV_W, D_FF), CONV_W ** -0.5),
        'b_conv': nrm(ks[17], (DEPTH, D_FF), 0.01),
        'w_down': nrm(ks[18], (DEPTH, D_FF, D_MODEL), D_FF ** -0.5),
        'g_ffn_post': gain(ks[19], D_MODEL),
    }


def reference(x_prompt, x_sample, cache_k, cache_v, state_gla, state_conv, g_mix_pre, w_in, w_gate_up,
              b_gate, rel_bias, g_gla, w_o, g_mix_post, g_ffn_pre, w_up, w_conv, b_conv, w_down, g_ffn_post):
    xp, xs = x_prompt, x_sample
    kp_l, vp_l, gp_l, cp_l, ks_l, vs_l, gs_l, cs_l = [], [], [], [], [], [], [], []
    for l in range(DEPTH):
        shared = (g_mix_pre[l], w_in[l], w_gate_up[l], b_gate[l], g_gla[l], w_o[l], g_mix_post[l],
                  g_ffn_pre[l], w_up[l], w_conv[l], b_conv[l], w_down[l], g_ffn_post[l])
        rb = rel_bias[l]
        bp = xp.shape[0]
        xp, kp, vp, gp, cp = encoder_layer(
            xp, lambda q, k, v, rb=rb: prompt_attention(q, k, v, rb),
            jnp.zeros((bp, N_HEADS_GLA, DK_GLA, DV_GLA), jnp.float32),
            jnp.zeros((bp, CONV_W - 1, D_FF), xp.dtype), *shared)
        keep = min(BAND_PAST, xp.shape[1])
        kp_l.append(kp[:, -keep:]); vp_l.append(vp[:, -keep:]); gp_l.append(gp); cp_l.append(cp)
        ck, cv = cache_k[l], cache_v[l]
        xs, kn, vn, gn, cn = encoder_layer(
            xs, lambda q, k, v, ck=ck, cv=cv, rb=rb: sample_attention(q, k, v, ck, cv, rb),
            state_gla[l], state_conv[l], *shared)
        ks_l.append(kn); vs_l.append(vn); gs_l.append(gn); cs_l.append(cn)
    return (xp, xs, jnp.stack(kp_l), jnp.stack(vp_l), jnp.stack(gp_l), jnp.stack(cp_l),
            jnp.stack(ks_l), jnp.stack(vs_l), jnp.stack(gs_l), jnp.stack(cs_l))
```

```python
import functools

import jax
import jax.numpy as jnp
from jax import lax
from jax.experimental import pallas as pl
from jax.experimental.pallas import tpu as pltpu

D_MODEL = 2048
CHUNK = 64
N_PAST_CHUNKS = 8
BAND_PAST = CHUNK * N_PAST_CHUNKS
ATT_WIDTH = 1024
HEAD_DIM_ATT = 128
N_HEADS_ATT = 8
MAX_REL = 128
GLA_V_WIDTH = 1024
N_HEADS_GLA = 4
DV_GLA = 256
DK_GLA = 128
GLA_QK_WIDTH = 512
GATE_RANK = 16
GATE_TAU = 16.0
D_FF = 5632
CONV_W = 3
EPS = 1e-6
MAIN_WIDTH = 3 * ATT_WIDTH + 2 * GLA_QK_WIDTH + 2 * GLA_V_WIDTH

LANES = 128
VMEM_LIMIT_BYTES = 56 * 1024 * 1024
NEG_INF = -1e30

F32 = jnp.float32
BF16 = jnp.bfloat16

NT_DIMS = (((1,), (1,)), ((), ()))
TN_DIMS = (((0,), (0,)), ((), ()))


def _rms(x, g):
    return x * lax.rsqrt(jnp.mean(x * x, axis=-1, keepdims=True) + EPS) * g


def _params(*sem):
    return pltpu.CompilerParams(dimension_semantics=sem, vmem_limit_bytes=VMEM_LIMIT_BYTES)


IN_TN = 1024


def _inproj_kernel(x_ref, g_ref, w_ref, wlo_ref, proj_ref, kv_ref, alo_ref, h_ref):
    j = pl.program_id(1)

    @pl.when(j == 0)
    def _():
        h = _rms(x_ref[...], g_ref[...]).astype(BF16)
        h_ref[...] = h
        alo_ref[...] = jnp.dot(h, wlo_ref[...], preferred_element_type=F32).astype(alo_ref.dtype)

    acc = jnp.dot(h_ref[...], w_ref[...], preferred_element_type=F32)
    proj_ref[...] = acc.astype(proj_ref.dtype)

    @pl.when((j == 1) | (j == 2))
    def _():
        kv_ref[...] = acc


def _in_proj(x, g, w_main, w_lo, tm):
    m = x.shape[0]
    n_tiles = MAIN_WIDTH // IN_TN
    return pl.pallas_call(
        _inproj_kernel,
        grid=(m // tm, n_tiles),
        in_specs=[
            pl.BlockSpec((tm, D_MODEL), lambda i, j: (i, 0)),
            pl.BlockSpec((1, D_MODEL), lambda i, j: (0, 0)),
            pl.BlockSpec((D_MODEL, IN_TN), lambda i, j: (0, j)),
            pl.BlockSpec((D_MODEL, LANES), lambda i, j: (0, 0)),
        ],
        out_specs=[
            pl.BlockSpec((tm, IN_TN), lambda i, j: (i, j)),
            pl.BlockSpec((tm, IN_TN), lambda i, j: (i, jnp.clip(j - 1, 0, 1))),
            pl.BlockSpec((tm, LANES), lambda i, j: (i, 0)),
        ],
        out_shape=[
            jax.ShapeDtypeStruct((m, MAIN_WIDTH), BF16),
            jax.ShapeDtypeStruct((m, 2 * ATT_WIDTH), F32),
            jax.ShapeDtypeStruct((m, LANES), BF16),
        ],
        scratch_shapes=[pltpu.VMEM((tm, D_MODEL), BF16)],
        compiler_params=_params("parallel", "arbitrary"),
        name="in_proj",
    )(x, g, w_main, w_lo)


ATT_TQ = 2 * CHUNK
ATT_KWIN = BAND_PAST + ATT_TQ
ATT_GPAD = 768


def _toeplitz(g_row, tq, width):
    gpad = g_row.shape[-1]
    rolled = pltpu.roll(jnp.broadcast_to(g_row, (tq, gpad)), gpad - (tq - 1), axis=1, stride=1, stride_axis=0)
    return rolled[:, :width]


def _attn_prompt_kernel(q_ref, k_ref, v_ref, g_ref, o_ref, kp_ref, vp_ref, bias_ref):
    t_len = q_ref.shape[0]
    zeros = jnp.zeros((BAND_PAST, HEAD_DIM_ATT), BF16)
    kp_ref[0:BAND_PAST, :] = zeros
    vp_ref[0:BAND_PAST, :] = zeros
    kp_ref[BAND_PAST:, :] = k_ref[...]
    vp_ref[BAND_PAST:, :] = v_ref[...]

    qi = lax.broadcasted_iota(jnp.int32, (ATT_TQ, ATT_KWIN), 0)
    kj = lax.broadcasted_iota(jnp.int32, (ATT_TQ, ATT_KWIN), 1)
    qc = qi // CHUNK
    kc = kj // CHUNK
    in_band = (kc >= qc) & (kc <= qc + N_PAST_CHUNKS)
    bias_ref[...] = jnp.where(in_band, _toeplitz(g_ref[0], ATT_TQ, ATT_KWIN), NEG_INF)

    scale = HEAD_DIM_ATT ** -0.5

    def body(t, carry):
        r0 = pl.multiple_of(t * ATT_TQ, ATT_TQ)
        q = q_ref[pl.ds(r0, ATT_TQ), :]
        kw = kp_ref[pl.ds(r0, ATT_KWIN), :]
        vw = vp_ref[pl.ds(r0, ATT_KWIN), :]
        s = lax.dot_general(q, kw, NT_DIMS, preferred_element_type=F32) * scale + bias_ref[...]
        s = jnp.where(kj + (r0 - BAND_PAST) >= 0, s, NEG_INF)
        m = jnp.max(s, axis=-1, keepdims=True)
        p = jnp.exp(s - m)
        l = jnp.sum(p, axis=-1, keepdims=True)
        o = jnp.dot(p.astype(BF16), vw, preferred_element_type=F32) / l
        o_ref[pl.ds(r0, ATT_TQ), :] = o.astype(o_ref.dtype)
        return carry

    lax.fori_loop(0, t_len // ATT_TQ, body, 0)


def _attn_prompt(proj, g_tab, n_batch, t_len):
    m = proj.shape[0]
    blk = (t_len, HEAD_DIM_ATT)
    return pl.pallas_call(
        _attn_prompt_kernel,
        grid=(n_batch, N_HEADS_ATT),
        in_specs=[
            pl.BlockSpec(blk, lambda b, h: (b, h)),
            pl.BlockSpec(blk, lambda b, h: (b, N_HEADS_ATT + h)),
            pl.BlockSpec(blk, lambda b, h: (b, 2 * N_HEADS_ATT + h)),
            pl.BlockSpec((1, 1, ATT_GPAD), lambda b, h: (h, 0, 0)),
        ],
        out_specs=pl.BlockSpec(blk, lambda b, h: (b, h)),
        out_shape=jax.ShapeDtypeStruct((m, ATT_WIDTH), BF16),
        scratch_shapes=[
            pltpu.VMEM((BAND_PAST + t_len, HEAD_DIM_ATT), BF16),
            pltpu.VMEM((BAND_PAST + t_len, HEAD_DIM_ATT), BF16),
            pltpu.VMEM((ATT_TQ, ATT_KWIN), F32),
        ],
        compiler_params=_params("parallel", "arbitrary"),
        name="attn_prompt",
    )(proj, proj, proj, g_tab)


SAMPLE_GPAD = 640


def _attn_sample_kernel(q_ref, kn_ref, vn_ref, ck_ref, cv_ref, g_ref, o_ref):
    tq = q_ref.shape[0]
    n_cache = ck_ref.shape[1]
    scale = HEAD_DIM_ATT ** -0.5
    for h in range(N_HEADS_ATT):
        sl = slice(h * HEAD_DIM_ATT, (h + 1) * HEAD_DIM_ATT)
        bias = _toeplitz(g_ref[h], tq, n_cache + LANES)
        q = q_ref[:, sl]
        ck = ck_ref[0, :, sl].astype(BF16)
        cv = cv_ref[0, :, sl].astype(BF16)
        s1 = lax.dot_general(q, ck, NT_DIMS, preferred_element_type=F32) * scale + bias[:, :n_cache]
        s2 = lax.dot_general(q, kn_ref[:, sl], NT_DIMS, preferred_element_type=F32) * scale + bias[:, n_cache:n_cache + tq]
        m = jnp.maximum(jnp.max(s1, axis=-1, keepdims=True), jnp.max(s2, axis=-1, keepdims=True))
        p1 = jnp.exp(s1 - m)
        p2 = jnp.exp(s2 - m)
        l = jnp.sum(p1, axis=-1, keepdims=True) + jnp.sum(p2, axis=-1, keepdims=True)
        o = jnp.dot(p1.astype(BF16), cv, preferred_element_type=F32)
        o = o + jnp.dot(p2.astype(BF16), vn_ref[:, sl], preferred_element_type=F32)
        o_ref[:, sl] = (o / l).astype(o_ref.dtype)


def _attn_sample(proj, cache_k, cache_v, g_tab, n_streams, t_len):
    m = proj.shape[0]
    n_cache = cache_k.shape[1]
    blk = (t_len, ATT_WIDTH)
    return pl.pallas_call(
        _attn_sample_kernel,
        grid=(n_streams,),
        in_specs=[
            pl.BlockSpec(blk, lambda s: (s, 0)),
            pl.BlockSpec(blk, lambda s: (s, 1)),
            pl.BlockSpec(blk, lambda s: (s, 2)),
            pl.BlockSpec((1, n_cache, ATT_WIDTH), lambda s: (s, 0, 0)),
            pl.BlockSpec((1, n_cache, ATT_WIDTH), lambda s: (s, 0, 0)),
            pl.BlockSpec((N_HEADS_ATT, 1, SAMPLE_GPAD), lambda s: (0, 0, 0)),
        ],
        out_specs=pl.BlockSpec(blk, lambda s: (s, 0)),
        out_shape=jax.ShapeDtypeStruct((m, ATT_WIDTH), BF16),
        compiler_params=_params("parallel"),
        name="attn_sample",
    )(proj, proj, proj, cache_k, cache_v, g_tab)


def _gla_kernel(q_ref, k_ref, v_ref, r_ref, alo_ref, wg_ref, bg_ref, gg_ref, s0_ref,
                o_ref, s_out_ref, loga_ref, st_ref, *, chunk):
    t_len = q_ref.shape[0]
    x = jnp.dot(alo_ref[...], wg_ref[...], preferred_element_type=F32) + bg_ref[...]
    loga_ref[...] = (jnp.minimum(x, 0.0) - jnp.log1p(jnp.exp(-jnp.abs(x)))) * (1.0 / GATE_TAU)

    for h in range(N_HEADS_GLA):
        st_ref[h] = s0_ref[0, h].T

    rows = lax.broadcasted_iota(jnp.int32, (chunk, GLA_QK_WIDTH), 0)
    tril = (lax.broadcasted_iota(jnp.int32, (chunk, chunk), 0)
            >= lax.broadcasted_iota(jnp.int32, (chunk, chunk), 1))
    mid_row = chunk // 2 - 1

    def body(ci, carry):
        r0 = pl.multiple_of(ci * chunk, chunk)
        c = loga_ref[pl.ds(r0, chunk), :]
        step = 1
        while step < chunk:
            c = c + jnp.where(rows >= step, pltpu.roll(c, step, axis=0), 0.0)
            step *= 2
        mid = c[mid_row:mid_row + 1, :]
        last = c[chunk - 1:chunk, :]
        e_q = jnp.exp(c - mid)
        e_k = jnp.exp(mid - c)
        e_in = jnp.exp(c)
        e_out = jnp.exp(last - c)
        decay = jnp.exp(last)
        q = q_ref[pl.ds(r0, chunk), :].astype(F32) * (DK_GLA ** -0.5)
        k = k_ref[pl.ds(r0, chunk), :].astype(F32)
        for h in range(N_HEADS_GLA):
            sl = slice(h * DK_GLA, (h + 1) * DK_GLA)
            vl = slice(h * DV_GLA, (h + 1) * DV_GLA)
            v = v_ref[pl.ds(r0, chunk), vl]
            a = lax.dot_general((q[:, sl] * e_q[:, sl]).astype(BF16), (k[:, sl] * e_k[:, sl]).astype(BF16),
                                NT_DIMS, preferred_element_type=F32)
            a = jnp.where(tril, a, 0.0)
            st = st_ref[h]
            o = jnp.dot(a.astype(BF16), v, preferred_element_type=F32)
            o = o + lax.dot_general((q[:, sl] * e_in[:, sl]).astype(BF16), st.astype(BF16), NT_DIMS,
                                    preferred_element_type=F32)
            k_out = (k[:, sl] * e_out[:, sl]).astype(BF16)
            st_ref[h] = st * decay[:, sl] + lax.dot_general(v, k_out, TN_DIMS, preferred_element_type=F32)
            r = r_ref[pl.ds(r0, chunk), vl].astype(F32)
            o_ref[pl.ds(r0, chunk), vl] = (_rms(o, gg_ref[...]) * (r * jax.nn.sigmoid(r))).astype(o_ref.dtype)
        return carry

    lax.fori_loop(0, t_len // chunk, body, 0)

    for h in range(N_HEADS_GLA):
        s_out_ref[0, h] = st_ref[h].T


def _gla(proj, a_lo, w_gate, b_gate, g_gla, state0, n_streams, t_len):
    m = proj.shape[0]
    chunk = min(32, t_len)
    qk_blk = (t_len, GLA_QK_WIDTH)
    v_blk = (t_len, GLA_V_WIDTH)
    qk0 = 3 * ATT_WIDTH // GLA_QK_WIDTH
    v0 = (3 * ATT_WIDTH + 2 * GLA_QK_WIDTH) // GLA_V_WIDTH
    state_blk = (1, N_HEADS_GLA, DK_GLA, DV_GLA)
    return pl.pallas_call(
        functools.partial(_gla_kernel, chunk=chunk),
        grid=(n_streams,),
        in_specs=[
            pl.BlockSpec(qk_blk, lambda s: (s, qk0)),
            pl.BlockSpec(qk_blk, lambda s: (s, qk0 + 1)),
            pl.BlockSpec(v_blk, lambda s: (s, v0)),
            pl.BlockSpec(v_blk, lambda s: (s, v0 + 1)),
            pl.BlockSpec((t_len, LANES), lambda s: (s, 0)),
            pl.BlockSpec((LANES, GLA_QK_WIDTH), lambda s: (0, 0)),
            pl.BlockSpec((1, GLA_QK_WIDTH), lambda s: (0, 0)),
            pl.BlockSpec((1, DV_GLA), lambda s: (0, 0)),
            pl.BlockSpec(state_blk, lambda s: (s, 0, 0, 0)),
        ],
        out_specs=[
            pl.BlockSpec(v_blk, lambda s: (s, 0)),
            pl.BlockSpec(state_blk, lambda s: (s, 0, 0, 0)),
        ],
        out_shape=[
            jax.ShapeDtypeStruct((m, GLA_V_WIDTH), BF16),
            jax.ShapeDtypeStruct((n_streams, N_HEADS_GLA, DK_GLA, DV_GLA), F32),
        ],
        scratch_shapes=[
            pltpu.VMEM((t_len, GLA_QK_WIDTH), F32),
            pltpu.VMEM((N_HEADS_GLA, DV_GLA, DK_GLA), F32),
        ],
        compiler_params=_params("parallel"),
        name="gla",
    )(proj, proj, proj, proj, a_lo, w_gate, b_gate, g_gla, state0)


def _outproj_kernel(oa_ref, og_ref, wa_ref, wg_ref, x_ref, gpost_ref, gpre_ref, x1_ref, h2_ref):
    mix = jnp.dot(oa_ref[...], wa_ref[...], preferred_element_type=F32)
    mix = mix + jnp.dot(og_ref[...], wg_ref[...], preferred_element_type=F32)
    x1 = x_ref[...] + _rms(mix, gpost_ref[...])
    x1_ref[...] = x1
    h2_ref[...] = _rms(x1, gpre_ref[...]).astype(h2_ref.dtype)


def _out_proj(o_att, o_gla, w_o, x, g_post, g_pre, tm):
    m = x.shape[0]
    row = lambda i: (i, 0)
    const = lambda i: (0, 0)
    return pl.pallas_call(
        _outproj_kernel,
        grid=(m // tm,),
        in_specs=[
            pl.BlockSpec((tm, ATT_WIDTH), row),
            pl.BlockSpec((tm, GLA_V_WIDTH), row),
            pl.BlockSpec((ATT_WIDTH, D_MODEL), const),
            pl.BlockSpec((GLA_V_WIDTH, D_MODEL), lambda i: (1, 0)),
            pl.BlockSpec((tm, D_MODEL), row),
            pl.BlockSpec((1, D_MODEL), const),
            pl.BlockSpec((1, D_MODEL), const),
        ],
        out_specs=[pl.BlockSpec((tm, D_MODEL), row), pl.BlockSpec((tm, D_MODEL), row)],
        out_shape=[jax.ShapeDtypeStruct((m, D_MODEL), F32), jax.ShapeDtypeStruct((m, D_MODEL), BF16)],
        compiler_params=_params("parallel"),
        name="out_proj",
    )(o_att, o_gla, w_o, w_o, x, g_post, g_pre)


FFN_TN = 256


def _ffn_up_kernel(h_ref, wg_ref, wv_ref, p0_ref, p1_ref, wc_ref, bc_ref, act_ref, cs_ref, *, t_len):
    tm = h_ref.shape[0]
    n_streams = tm // t_len
    h = h_ref[...]
    gate = jnp.dot(h, wg_ref[...], preferred_element_type=F32)
    val = jnp.dot(h, wv_ref[...], preferred_element_type=F32)
    pos = lax.broadcasted_iota(jnp.int32, gate.shape, 0) & (t_len - 1)
    p0 = p0_ref[0]
    p1 = p1_ref[0]
    back1 = jnp.where(pos == 0, p1, pltpu.roll(gate, 1, axis=0))
    back2 = jnp.where(pos == 0, p0, jnp.where(pos == 1, p1, pltpu.roll(gate, 2, axis=0)))
    conv = bc_ref[...] + (back2 * wc_ref[0:1, :] + back1 * wc_ref[1:2, :] + gate * wc_ref[2:3, :])
    act_ref[...] = (conv * jax.nn.sigmoid(conv) * val).astype(act_ref.dtype)
    cs_ref[...] = gate.reshape(n_streams, t_len, gate.shape[1])[:, t_len - (CONV_W - 1):, :]


def _ffn_up(h2, w_up, prev0, prev1, w_conv, b_conv, t_len, streams_per_tile):
    m = h2.shape[0]
    tm = t_len * streams_per_tile
    n_streams = m // t_len
    n_tiles = D_FF // FFN_TN
    prev_rows = prev0.shape[1]
    return pl.pallas_call(
        functools.partial(_ffn_up_kernel, t_len=t_len),
        grid=(m // tm, n_tiles),
        in_specs=[
            pl.BlockSpec((tm, D_MODEL), lambda i, j: (i, 0)),
            pl.BlockSpec((D_MODEL, FFN_TN), lambda i, j: (0, j)),
            pl.BlockSpec((D_MODEL, FFN_TN), lambda i, j: (0, n_tiles + j)),
            pl.BlockSpec((1, prev_rows, FFN_TN), lambda i, j: (i, 0, j)),
            pl.BlockSpec((1, prev_rows, FFN_TN), lambda i, j: (i, 0, j)),
            pl.BlockSpec((CONV_W, FFN_TN), lambda i, j: (0, j)),
            pl.BlockSpec((1, FFN_TN), lambda i, j: (0, j)),
        ],
        out_specs=[
            pl.BlockSpec((tm, FFN_TN), lambda i, j: (i, j)),
            pl.BlockSpec((streams_per_tile, CONV_W - 1, FFN_TN), lambda i, j: (i, 0, j)),
        ],
        out_shape=[
            jax.ShapeDtypeStruct((m, D_FF), BF16),
            jax.ShapeDtypeStruct((n_streams, CONV_W - 1, D_FF), F32),
        ],
        compiler_params=_params("parallel", "arbitrary"),
        name="ffn_up",
    )(h2, w_up, w_up, prev0, prev1, w_conv, b_conv)


DOWN_TK = 512


def _ffn_down_kernel(a_ref, w_ref, x1_ref, g_ref, o_ref, acc_ref):
    kk = pl.program_id(1)

    @pl.when(kk == 0)
    def _():
        acc_ref[...] = jnp.zeros_like(acc_ref)

    acc_ref[...] += jnp.dot(a_ref[...], w_ref[...], preferred_element_type=F32)

    @pl.when(kk == pl.num_programs(1) - 1)
    def _():
        o_ref[...] = x1_ref[...] + _rms(acc_ref[...], g_ref[...])


def _ffn_down(act, w_down, x1, g_post, tm):
    m = act.shape[0]
    return pl.pallas_call(
        _ffn_down_kernel,
        grid=(m // tm, D_FF // DOWN_TK),
        in_specs=[
            pl.BlockSpec((tm, DOWN_TK), lambda i, k: (i, k)),
            pl.BlockSpec((DOWN_TK, D_MODEL), lambda i, k: (k, 0)),
            pl.BlockSpec((tm, D_MODEL), lambda i, k: (i, 0)),
            pl.BlockSpec((1, D_MODEL), lambda i, k: (0, 0)),
        ],
        out_specs=pl.BlockSpec((tm, D_MODEL), lambda i, k: (i, 0)),
        out_shape=jax.ShapeDtypeStruct((m, D_MODEL), F32),
        scratch_shapes=[pltpu.VMEM((tm, D_MODEL), F32)],
        compiler_params=_params("parallel", "arbitrary"),
        name="ffn_down",
    )(act, w_down, x1, g_post)


def _bias_diagonals(rel_bias, tq, width):
    rel = BAND_PAST + (tq - 1) - jnp.arange(width)
    idx = jnp.clip(rel, -MAX_REL, MAX_REL) + MAX_REL
    return rel_bias[:, idx].reshape(N_HEADS_ATT, 1, width)


def _layer(x, n_streams, t_len, att_fn, gla_state0, prev0, prev1, streams_per_tile, w, row_tile):
    m = n_streams * t_len
    x2 = x.reshape(m, D_MODEL)
    proj, kv, a_lo = _in_proj(x2, w["g_mix_pre"], w["w_main"], w["w_lo"], row_tile)
    o_att = att_fn(proj)
    o_gla, s_gla = _gla(proj, a_lo, w["w_gate"], w["b_gate"], w["g_gla"], gla_state0, n_streams, t_len)
    x1, h2 = _out_proj(o_att, o_gla, w["w_o"], x2, w["g_mix_post"], w["g_ffn_pre"], row_tile)
    act, conv_state = _ffn_up(h2, w["w_up"], prev0, prev1, w["w_conv"], w["b_conv"], t_len, streams_per_tile)
    y = _ffn_down(act, w["w_down"], x1, w["g_ffn_post"], row_tile)
    k_new = kv[:, :ATT_WIDTH].reshape(n_streams, t_len, N_HEADS_ATT, HEAD_DIM_ATT)
    v_new = kv[:, ATT_WIDTH:].reshape(n_streams, t_len, N_HEADS_ATT, HEAD_DIM_ATT)
    return y.reshape(n_streams, t_len, D_MODEL), k_new, v_new, s_gla, conv_state


def kernel(x_prompt, x_sample, cache_k, cache_v, state_gla, state_conv, g_mix_pre, w_in, w_gate_up, b_gate,
           rel_bias, g_gla, w_o, g_mix_post, g_ffn_pre, w_up, w_conv, b_conv, w_down, g_ffn_post):
    depth = w_in.shape[0]
    n_prompt, t_prompt, _ = x_prompt.shape
    n_sample, t_sample, _ = x_sample.shape
    xp, xs = x_prompt, x_sample
    outs = [[] for _ in range(8)]
    for l in range(depth):
        w_lo = jnp.pad(w_in[l][:, MAIN_WIDTH:], ((0, 0), (0, LANES - GATE_RANK)))
        w = {
            "g_mix_pre": g_mix_pre[l][None],
            "w_main": w_in[l][:, :MAIN_WIDTH].astype(BF16),
            "w_lo": w_lo.astype(BF16),
            "w_gate": jnp.pad(w_gate_up[l], ((0, LANES - GATE_RANK), (0, 0))).astype(BF16),
            "b_gate": b_gate[l][None],
            "g_gla": g_gla[l][None],
            "w_o": w_o[l].astype(BF16),
            "g_mix_post": g_mix_post[l][None],
            "g_ffn_pre": g_ffn_pre[l][None],
            "w_up": w_up[l].astype(BF16),
            "w_conv": w_conv[l],
            "b_conv": b_conv[l][None],
            "w_down": w_down[l].astype(BF16),
            "g_ffn_post": g_ffn_post[l][None],
        }
        g_prompt = _bias_diagonals(rel_bias[l], ATT_TQ, ATT_GPAD)
        zero_prev = jnp.zeros((n_prompt, 1, D_FF), F32)
        xp, kp, vp, gp, cp = _layer(
            xp, n_prompt, t_prompt,
            lambda proj: _attn_prompt(proj, g_prompt, n_prompt, t_prompt),
            jnp.zeros((n_prompt, N_HEADS_GLA, DK_GLA, DV_GLA), F32),
            zero_prev, zero_prev, 1, w, 512)
        keep = min(BAND_PAST, t_prompt)
        for lst, val in zip(outs[:4], (kp[:, -keep:], vp[:, -keep:], gp, cp)):
            lst.append(val)
        g_sample = _bias_diagonals(rel_bias[l], t_sample, SAMPLE_GPAD)
        n_cache = cache_k.shape[2]
        ck = cache_k[l].reshape(n_sample, n_cache, ATT_WIDTH)
        cv = cache_v[l].reshape(n_sample, n_cache, ATT_WIDTH)
        prev0 = jnp.repeat(state_conv[l][:, 0], t_sample, axis=0)[None]
        prev1 = jnp.repeat(state_conv[l][:, 1], t_sample, axis=0)[None]
        xs, kn, vn, gn, cn = _layer(
            xs, n_sample, t_sample,
            lambda proj: _attn_sample(proj, ck, cv, g_sample, n_sample, t_sample),
            state_gla[l], prev0, prev1, n_sample, w, n_sample * t_sample)
        for lst, val in zip(outs[4:], (kn, vn, gn, cn)):
            lst.append(val)
    return (xp, xs) + tuple(jnp.stack(o) for o in outs)
```

```python
import functools

import jax
import jax.numpy as jnp
from jax import lax
from jax.experimental import pallas as pl
from jax.experimental.pallas import tpu as pltpu

D_MODEL = 2048
CHUNK = 64
N_PAST_CHUNKS = 8
BAND_PAST = CHUNK * N_PAST_CHUNKS
ATT_WIDTH = 1024
HEAD_DIM_ATT = 128
N_HEADS_ATT = 8
MAX_REL = 128
GLA_V_WIDTH = 1024
N_HEADS_GLA = 4
DV_GLA = 256
DK_GLA = 128
GLA_QK_WIDTH = 512
GATE_RANK = 16
GATE_TAU = 16.0
D_FF = 5632
CONV_W = 3
EPS = 1e-6
MAIN_WIDTH = 3 * ATT_WIDTH + 2 * GLA_QK_WIDTH + 2 * GLA_V_WIDTH

LANES = 128
VMEM_LIMIT_BYTES = 56 * 1024 * 1024
NEG_INF = -1e30

F32 = jnp.float32
BF16 = jnp.bfloat16

NT_DIMS = (((1,), (1,)), ((), ()))
TN_DIMS = (((0,), (0,)), ((), ()))


def _rms(x, g):
    return x * lax.rsqrt(jnp.mean(x * x, axis=-1, keepdims=True) + EPS) * g


def _params(*sem):
    return pltpu.CompilerParams(dimension_semantics=sem, vmem_limit_bytes=VMEM_LIMIT_BYTES)


IN_TN = 1024


def _inproj_kernel(x_ref, g_ref, w_ref, wlo_ref, proj_ref, k_ref, v_ref, alo_ref, h_ref):
    j = pl.program_id(1)

    @pl.when(j == 0)
    def _():
        h = _rms(x_ref[...], g_ref[...]).astype(BF16)
        h_ref[...] = h
        alo_ref[...] = jnp.dot(h, wlo_ref[...], preferred_element_type=F32).astype(alo_ref.dtype)

    acc = jnp.dot(h_ref[...], w_ref[...], preferred_element_type=F32)
    proj_ref[...] = acc.astype(proj_ref.dtype)

    @pl.when(j == 1)
    def _():
        k_ref[...] = acc[acc.shape[0] - k_ref.shape[0]:, :]

    @pl.when(j == 2)
    def _():
        v_ref[...] = acc[acc.shape[0] - v_ref.shape[0]:, :]


def _in_proj(x, g, w_main, w_lo, tm, kv_rows, kv_group):
    m = x.shape[0]
    n_tiles = MAIN_WIDTH // IN_TN
    kv_spec = pl.BlockSpec((kv_rows, ATT_WIDTH), lambda i, j: (i // kv_group, 0))
    kv_shape = jax.ShapeDtypeStruct((m // (tm * kv_group) * kv_rows, ATT_WIDTH), F32)
    return pl.pallas_call(
        _inproj_kernel,
        grid=(m // tm, n_tiles),
        in_specs=[
            pl.BlockSpec((tm, D_MODEL), lambda i, j: (i, 0)),
            pl.BlockSpec((1, D_MODEL), lambda i, j: (0, 0)),
            pl.BlockSpec((D_MODEL, IN_TN), lambda i, j: (0, j)),
            pl.BlockSpec((D_MODEL, LANES), lambda i, j: (0, 0)),
        ],
        out_specs=[
            pl.BlockSpec((tm, IN_TN), lambda i, j: (i, j)),
            kv_spec,
            kv_spec,
            pl.BlockSpec((tm, LANES), lambda i, j: (i, 0)),
        ],
        out_shape=[
            jax.ShapeDtypeStruct((m, MAIN_WIDTH), BF16),
            kv_shape,
            kv_shape,
            jax.ShapeDtypeStruct((m, LANES), BF16),
        ],
        scratch_shapes=[pltpu.VMEM((tm, D_MODEL), BF16)],
        compiler_params=_params("arbitrary", "arbitrary"),
        name="in_proj",
    )(x, g, w_main, w_lo)


ATT_TQ = 2 * CHUNK
ATT_KWIN = BAND_PAST + ATT_TQ
ATT_GPAD = 768
ATT_UNROLL = 4


def _toeplitz(g_row, tq, width):
    gpad = g_row.shape[-1]
    rolled = pltpu.roll(jnp.broadcast_to(g_row, (tq, gpad)), gpad - (tq - 1), axis=1, stride=1, stride_axis=0)
    return rolled[:, :width]


def _attn_prompt_kernel(q_ref, k_ref, v_ref, g_ref, o_ref, kp_ref, vp_ref, bias_ref):
    t_len = q_ref.shape[0]
    zeros = jnp.zeros((BAND_PAST, HEAD_DIM_ATT), BF16)
    kp_ref[0:BAND_PAST, :] = zeros
    vp_ref[0:BAND_PAST, :] = zeros
    kp_ref[BAND_PAST:, :] = k_ref[...]
    vp_ref[BAND_PAST:, :] = v_ref[...]

    qi = lax.broadcasted_iota(jnp.int32, (ATT_TQ, ATT_KWIN), 0)
    kj = lax.broadcasted_iota(jnp.int32, (ATT_TQ, ATT_KWIN), 1)
    qc = qi // CHUNK
    kc = kj // CHUNK
    in_band = (kc >= qc) & (kc <= qc + N_PAST_CHUNKS)
    bias_ref[...] = jnp.where(in_band, _toeplitz(g_ref[0], ATT_TQ, ATT_KWIN), NEG_INF)

    scale = HEAD_DIM_ATT ** -0.5

    def body(it, carry, *, has_padding):
        starts = [pl.multiple_of((it * ATT_UNROLL + u) * ATT_TQ, ATT_TQ) for u in range(ATT_UNROLL)]
        scores = [lax.dot_general(q_ref[pl.ds(r0, ATT_TQ), :], kp_ref[pl.ds(r0, ATT_KWIN), :], NT_DIMS,
                                  preferred_element_type=F32) for r0 in starts]
        probs, sums = [], []
        for r0, s in zip(starts, scores):
            s = s * scale + bias_ref[...]
            if has_padding:
                s = jnp.where(kj + (r0 - BAND_PAST) >= 0, s, NEG_INF)
            p = jnp.exp(s - jnp.max(s, axis=-1, keepdims=True))
            sums.append(jnp.sum(p, axis=-1, keepdims=True))
            probs.append(p.astype(BF16))
        for r0, p, l in zip(starts, probs, sums):
            o = jnp.dot(p, vp_ref[pl.ds(r0, ATT_KWIN), :], preferred_element_type=F32) / l
            o_ref[pl.ds(r0, ATT_TQ), :] = o.astype(o_ref.dtype)
        return carry

    n_iters = t_len // (ATT_TQ * ATT_UNROLL)
    n_padded = min(BAND_PAST // (ATT_TQ * ATT_UNROLL), n_iters)
    lax.fori_loop(0, n_padded, functools.partial(body, has_padding=True), 0)
    lax.fori_loop(n_padded, n_iters, functools.partial(body, has_padding=False), 0)


def _attn_prompt(proj, g_tab, n_batch, t_len):
    m = proj.shape[0]
    blk = (t_len, HEAD_DIM_ATT)
    return pl.pallas_call(
        _attn_prompt_kernel,
        grid=(n_batch, N_HEADS_ATT),
        in_specs=[
            pl.BlockSpec(blk, lambda b, h: (b, h)),
            pl.BlockSpec(blk, lambda b, h: (b, N_HEADS_ATT + h)),
            pl.BlockSpec(blk, lambda b, h: (b, 2 * N_HEADS_ATT + h)),
            pl.BlockSpec((1, 1, ATT_GPAD), lambda b, h: (h, 0, 0)),
        ],
        out_specs=pl.BlockSpec(blk, lambda b, h: (b, h)),
        out_shape=jax.ShapeDtypeStruct((m, ATT_WIDTH), BF16),
        scratch_shapes=[
            pltpu.VMEM((BAND_PAST + t_len, HEAD_DIM_ATT), BF16),
            pltpu.VMEM((BAND_PAST + t_len, HEAD_DIM_ATT), BF16),
            pltpu.VMEM((ATT_TQ, ATT_KWIN), F32),
        ],
        compiler_params=_params("parallel", "arbitrary"),
        name="attn_prompt",
    )(proj, proj, proj, g_tab)


SAMPLE_GPAD = 640


def _attn_sample_kernel(q_ref, kn_ref, vn_ref, ck_ref, cv_ref, g_ref, o_ref):
    tq = q_ref.shape[0]
    n_cache = ck_ref.shape[1]
    scale = HEAD_DIM_ATT ** -0.5
    for h in range(N_HEADS_ATT):
        sl = slice(h * HEAD_DIM_ATT, (h + 1) * HEAD_DIM_ATT)
        bias = _toeplitz(g_ref[h], tq, n_cache + LANES)
        q = q_ref[:, sl]
        ck = ck_ref[0, :, h, :].astype(BF16)
        cv = cv_ref[0, :, h, :].astype(BF16)
        s1 = lax.dot_general(q, ck, NT_DIMS, preferred_element_type=F32) * scale + bias[:, :n_cache]
        s2 = lax.dot_general(q, kn_ref[:, sl], NT_DIMS, preferred_element_type=F32) * scale + bias[:, n_cache:n_cache + tq]
        m = jnp.maximum(jnp.max(s1, axis=-1, keepdims=True), jnp.max(s2, axis=-1, keepdims=True))
        p1 = jnp.exp(s1 - m)
        p2 = jnp.exp(s2 - m)
        l = jnp.sum(p1, axis=-1, keepdims=True) + jnp.sum(p2, axis=-1, keepdims=True)
        o = jnp.dot(p1.astype(BF16), cv, preferred_element_type=F32)
        o = o + jnp.dot(p2.astype(BF16), vn_ref[:, sl], preferred_element_type=F32)
        o_ref[:, sl] = (o / l).astype(o_ref.dtype)


def _attn_sample(proj, cache_k, cache_v, g_tab, n_streams, t_len):
    m = proj.shape[0]
    n_cache = cache_k.shape[1]
    blk = (t_len, ATT_WIDTH)
    return pl.pallas_call(
        _attn_sample_kernel,
        grid=(n_streams,),
        in_specs=[
            pl.BlockSpec(blk, lambda s: (s, 0)),
            pl.BlockSpec(blk, lambda s: (s, 1)),
            pl.BlockSpec(blk, lambda s: (s, 2)),
            pl.BlockSpec((1, n_cache, N_HEADS_ATT, HEAD_DIM_ATT), lambda s: (s, 0, 0, 0)),
            pl.BlockSpec((1, n_cache, N_HEADS_ATT, HEAD_DIM_ATT), lambda s: (s, 0, 0, 0)),
            pl.BlockSpec((N_HEADS_ATT, 1, SAMPLE_GPAD), lambda s: (0, 0, 0)),
        ],
        out_specs=pl.BlockSpec(blk, lambda s: (s, 0)),
        out_shape=jax.ShapeDtypeStruct((m, ATT_WIDTH), BF16),
        compiler_params=_params("parallel"),
        name="attn_sample",
    )(proj, proj, proj, cache_k, cache_v, g_tab)


GLA_UNROLL = 4


def _gla_kernel(q_ref, k_ref, v_ref, r_ref, alo_ref, wg_ref, bg_ref, gg_ref, s0_ref,
                o_ref, s_out_ref, loga_ref, st_ref, *, chunk):
    t_len = q_ref.shape[0]
    x = jnp.dot(alo_ref[...], wg_ref[...], preferred_element_type=F32) + bg_ref[...]
    loga_ref[...] = (jnp.minimum(x, 0.0) - jnp.log1p(jnp.exp(-jnp.abs(x)))) * (1.0 / GATE_TAU)

    for h in range(N_HEADS_GLA):
        st_ref[h] = s0_ref[0, h].T

    rows = lax.broadcasted_iota(jnp.int32, (chunk, GLA_QK_WIDTH), 0)
    tril = (lax.broadcasted_iota(jnp.int32, (chunk, chunk), 0)
            >= lax.broadcasted_iota(jnp.int32, (chunk, chunk), 1))
    mid_row = chunk // 2 - 1

    heads = range(N_HEADS_GLA)
    ksl = [slice(h * DK_GLA, (h + 1) * DK_GLA) for h in heads]
    vsl = [slice(h * DV_GLA, (h + 1) * DV_GLA) for h in heads]
    n_unroll = min(GLA_UNROLL, t_len // chunk)

    def body(it, carry):
        starts = [pl.multiple_of((it * n_unroll + u) * chunk, chunk) for u in range(n_unroll)]
        scores, kv_sums, q_ins, decays = [], [], [], []
        for r0 in starts:
            c = loga_ref[pl.ds(r0, chunk), :]
            step = 1
            while step < chunk:
                c = c + jnp.where(rows >= step, pltpu.roll(c, step, axis=0), 0.0)
                step *= 2
            mid = c[mid_row:mid_row + 1, :]
            last = c[chunk - 1:chunk, :]
            q = q_ref[pl.ds(r0, chunk), :].astype(F32) * (DK_GLA ** -0.5)
            k = k_ref[pl.ds(r0, chunk), :].astype(F32)
            q_mid = (q * jnp.exp(c - mid)).astype(BF16)
            k_mid = (k * jnp.exp(mid - c)).astype(BF16)
            k_out = (k * jnp.exp(last - c)).astype(BF16)
            q_ins.append((q * jnp.exp(c)).astype(BF16))
            decays.append(jnp.exp(last))
            scores.append([lax.dot_general(q_mid[:, ksl[h]], k_mid[:, ksl[h]], NT_DIMS, preferred_element_type=F32)
                           for h in heads])
            kv_sums.append([lax.dot_general(v_ref[pl.ds(r0, chunk), vsl[h]], k_out[:, ksl[h]], TN_DIMS,
                                            preferred_element_type=F32) for h in heads])
        states = [st_ref[h] for h in heads]
        for u, r0 in enumerate(starts):
            inter = [lax.dot_general(q_ins[u][:, ksl[h]], states[h].astype(BF16), NT_DIMS,
                                     preferred_element_type=F32) for h in heads]
            states = [states[h] * decays[u][:, ksl[h]] + kv_sums[u][h] for h in heads]
            for h in heads:
                a = jnp.where(tril, scores[u][h], 0.0).astype(BF16)
                o = jnp.dot(a, v_ref[pl.ds(r0, chunk), vsl[h]], preferred_element_type=F32) + inter[h]
                r = r_ref[pl.ds(r0, chunk), vsl[h]].astype(F32)
                o_ref[pl.ds(r0, chunk), vsl[h]] = (_rms(o, gg_ref[...]) * (r * jax.nn.sigmoid(r))).astype(o_ref.dtype)
        for h in heads:
            st_ref[h] = states[h]
        return carry

    lax.fori_loop(0, t_len // (chunk * n_unroll), body, 0)

    for h in range(N_HEADS_GLA):
        s_out_ref[0, h] = st_ref[h].T


def _gla(proj, a_lo, w_gate, b_gate, g_gla, state0, n_streams, t_len):
    m = proj.shape[0]
    chunk = min(32, t_len)
    qk_blk = (t_len, GLA_QK_WIDTH)
    v_blk = (t_len, GLA_V_WIDTH)
    qk0 = 3 * ATT_WIDTH // GLA_QK_WIDTH
    v0 = (3 * ATT_WIDTH + 2 * GLA_QK_WIDTH) // GLA_V_WIDTH
    state_blk = (1, N_HEADS_GLA, DK_GLA, DV_GLA)
    return pl.pallas_call(
        functools.partial(_gla_kernel, chunk=chunk),
        grid=(n_streams,),
        in_specs=[
            pl.BlockSpec(qk_blk, lambda s: (s, qk0)),
            pl.BlockSpec(qk_blk, lambda s: (s, qk0 + 1)),
            pl.BlockSpec(v_blk, lambda s: (s, v0)),
            pl.BlockSpec(v_blk, lambda s: (s, v0 + 1)),
            pl.BlockSpec((t_len, LANES), lambda s: (s, 0)),
            pl.BlockSpec((LANES, GLA_QK_WIDTH), lambda s: (0, 0)),
            pl.BlockSpec((1, GLA_QK_WIDTH), lambda s: (0, 0)),
            pl.BlockSpec((1, DV_GLA), lambda s: (0, 0)),
            pl.BlockSpec(state_blk, lambda s: (s, 0, 0, 0)),
        ],
        out_specs=[
            pl.BlockSpec(v_blk, lambda s: (s, 0)),
            pl.BlockSpec(state_blk, lambda s: (s, 0, 0, 0)),
        ],
        out_shape=[
            jax.ShapeDtypeStruct((m, GLA_V_WIDTH), BF16),
            jax.ShapeDtypeStruct((n_streams, N_HEADS_GLA, DK_GLA, DV_GLA), F32),
        ],
        scratch_shapes=[
            pltpu.VMEM((t_len, GLA_QK_WIDTH), F32),
            pltpu.VMEM((N_HEADS_GLA, DV_GLA, DK_GLA), F32),
        ],
        compiler_params=_params("parallel"),
        name="gla",
    )(proj, proj, proj, proj, a_lo, w_gate, b_gate, g_gla, state0)


def _outproj_kernel(oa_ref, og_ref, wa_ref, wg_ref, x_ref, gpost_ref, gpre_ref, x1_ref, h2_ref):
    mix = jnp.dot(oa_ref[...], wa_ref[...], preferred_element_type=F32)
    mix = mix + jnp.dot(og_ref[...], wg_ref[...], preferred_element_type=F32)
    x1 = x_ref[...] + _rms(mix, gpost_ref[...])
    x1_ref[...] = x1
    h2_ref[...] = _rms(x1, gpre_ref[...]).astype(h2_ref.dtype)


def _out_proj(o_att, o_gla, w_o, x, g_post, g_pre, tm):
    m = x.shape[0]
    row = lambda i: (i, 0)
    const = lambda i: (0, 0)
    return pl.pallas_call(
        _outproj_kernel,
        grid=(m // tm,),
        in_specs=[
            pl.BlockSpec((tm, ATT_WIDTH), row),
            pl.BlockSpec((tm, GLA_V_WIDTH), row),
            pl.BlockSpec((ATT_WIDTH, D_MODEL), const),
            pl.BlockSpec((GLA_V_WIDTH, D_MODEL), lambda i: (1, 0)),
            pl.BlockSpec((tm, D_MODEL), row),
            pl.BlockSpec((1, D_MODEL), const),
            pl.BlockSpec((1, D_MODEL), const),
        ],
        out_specs=[pl.BlockSpec((tm, D_MODEL), row), pl.BlockSpec((tm, D_MODEL), row)],
        out_shape=[jax.ShapeDtypeStruct((m, D_MODEL), F32), jax.ShapeDtypeStruct((m, D_MODEL), BF16)],
        compiler_params=_params("parallel"),
        name="out_proj",
    )(o_att, o_gla, w_o, w_o, x, g_post, g_pre)


FFN_TN = 256
FFN_SUB_ROWS = 1024


def _ffn_up_kernel(h_ref, wg_ref, wv_ref, p0_ref, p1_ref, wc_ref, bc_ref, act_ref, cs_ref, *, t_len, sub_rows):
    tm = h_ref.shape[0]
    n_sub = tm // sub_rows
    assert (n_sub == 1 and sub_rows % t_len == 0) or tm == t_len
    seg = min(t_len, sub_rows)
    p0 = p0_ref[0]
    p1 = p1_ref[0]
    for c in range(n_sub):
        rs = slice(c * sub_rows, (c + 1) * sub_rows)
        h = h_ref[rs, :]
        gate = jnp.dot(h, wg_ref[...], preferred_element_type=F32)
        val = jnp.dot(h, wv_ref[...], preferred_element_type=F32)
        pos = lax.broadcasted_iota(jnp.int32, gate.shape, 0) & (seg - 1)
        back1 = jnp.where(pos == 0, p1, pltpu.roll(gate, 1, axis=0))
        back2 = jnp.where(pos == 0, p0, jnp.where(pos == 1, p1, pltpu.roll(gate, 2, axis=0)))
        conv = bc_ref[...] + (back2 * wc_ref[0:1, :] + back1 * wc_ref[1:2, :] + gate * wc_ref[2:3, :])
        act_ref[rs, :] = (conv * jax.nn.sigmoid(conv) * val).astype(act_ref.dtype)
        p0 = gate[sub_rows - 2:sub_rows - 1, :]
        p1 = gate[sub_rows - 1:sub_rows, :]
    cs_ref[...] = gate.reshape(sub_rows // seg, seg, gate.shape[1])[:, seg - (CONV_W - 1):, :]


def _ffn_up(h2, w_up, prev0, prev1, w_conv, b_conv, t_len, streams_per_tile):
    m = h2.shape[0]
    tm = t_len * streams_per_tile
    n_streams = m // t_len
    n_tiles = D_FF // FFN_TN
    prev_rows = prev0.shape[1]
    return pl.pallas_call(
        functools.partial(_ffn_up_kernel, t_len=t_len, sub_rows=min(tm, FFN_SUB_ROWS)),
        grid=(m // tm, n_tiles),
        in_specs=[
            pl.BlockSpec((tm, D_MODEL), lambda i, j: (i, 0)),
            pl.BlockSpec((D_MODEL, FFN_TN), lambda i, j: (0, j)),
            pl.BlockSpec((D_MODEL, FFN_TN), lambda i, j: (0, n_tiles + j)),
            pl.BlockSpec((1, prev_rows, FFN_TN), lambda i, j: (i, 0, j)),
            pl.BlockSpec((1, prev_rows, FFN_TN), lambda i, j: (i, 0, j)),
            pl.BlockSpec((CONV_W, FFN_TN), lambda i, j: (0, j)),
            pl.BlockSpec((1, FFN_TN), lambda i, j: (0, j)),
        ],
        out_specs=[
            pl.BlockSpec((tm, FFN_TN), lambda i, j: (i, j)),
            pl.BlockSpec((streams_per_tile, CONV_W - 1, FFN_TN), lambda i, j: (i, 0, j)),
        ],
        out_shape=[
            jax.ShapeDtypeStruct((m, D_FF), BF16),
            jax.ShapeDtypeStruct((n_streams, CONV_W - 1, D_FF), F32),
        ],
        compiler_params=_params("parallel", "arbitrary"),
        name="ffn_up",
    )(h2, w_up, w_up, prev0, prev1, w_conv, b_conv)


DOWN_TK = 512


def _ffn_down_kernel(a_ref, w_ref, x1_ref, g_ref, o_ref, acc_ref):
    kk = pl.program_id(1)

    @pl.when(kk == 0)
    def _():
        acc_ref[...] = jnp.zeros_like(acc_ref)

    acc_ref[...] += jnp.dot(a_ref[...], w_ref[...], preferred_element_type=F32)

    @pl.when(kk == pl.num_programs(1) - 1)
    def _():
        o_ref[...] = x1_ref[...] + _rms(acc_ref[...], g_ref[...])


def _ffn_down(act, w_down, x1, g_post, tm):
    m = act.shape[0]
    return pl.pallas_call(
        _ffn_down_kernel,
        grid=(m // tm, D_FF // DOWN_TK),
        in_specs=[
            pl.BlockSpec((tm, DOWN_TK), lambda i, k: (i, k)),
            pl.BlockSpec((DOWN_TK, D_MODEL), lambda i, k: (k, 0)),
            pl.BlockSpec((tm, D_MODEL), lambda i, k: (i, 0)),
            pl.BlockSpec((1, D_MODEL), lambda i, k: (0, 0)),
        ],
        out_specs=pl.BlockSpec((tm, D_MODEL), lambda i, k: (i, 0)),
        out_shape=jax.ShapeDtypeStruct((m, D_MODEL), F32),
        scratch_shapes=[pltpu.VMEM((tm, D_MODEL), F32)],
        compiler_params=_params("parallel", "arbitrary"),
        name="ffn_down",
    )(act, w_down, x1, g_post)


PROMPT_ROW_TILE = 1024
PROMPT_MID_TILE = 512


def _bias_diagonals(rel_bias, tq, width):
    rel = BAND_PAST + (tq - 1) - jnp.arange(width)
    idx = jnp.clip(rel, -MAX_REL, MAX_REL) + MAX_REL
    return rel_bias[:, idx].reshape(N_HEADS_ATT, 1, width)


def _layer(x, n_streams, t_len, att_fn, gla_state0, prev0, prev1, streams_per_tile, w, row_tile, mid_tile,
           kv_rows, kv_group):
    m = n_streams * t_len
    x2 = x.reshape(m, D_MODEL)
    proj, k32, v32, a_lo = _in_proj(x2, w["g_mix_pre"], w["w_main"], w["w_lo"], row_tile, kv_rows, kv_group)
    o_att = att_fn(proj)
    o_gla, s_gla = _gla(proj, a_lo, w["w_gate"], w["b_gate"], w["g_gla"], gla_state0, n_streams, t_len)
    x1, h2 = _out_proj(o_att, o_gla, w["w_o"], x2, w["g_mix_post"], w["g_ffn_pre"], mid_tile)
    act, conv_state = _ffn_up(h2, w["w_up"], prev0, prev1, w["w_conv"], w["b_conv"], t_len, streams_per_tile)
    y = _ffn_down(act, w["w_down"], x1, w["g_ffn_post"], row_tile)
    kv_shape = (n_streams, k32.shape[0] // n_streams, N_HEADS_ATT, HEAD_DIM_ATT)
    return y.reshape(n_streams, t_len, D_MODEL), k32.reshape(kv_shape), v32.reshape(kv_shape), s_gla, conv_state


def kernel(x_prompt, x_sample, cache_k, cache_v, state_gla, state_conv, g_mix_pre, w_in, w_gate_up, b_gate,
           rel_bias, g_gla, w_o, g_mix_post, g_ffn_pre, w_up, w_conv, b_conv, w_down, g_ffn_post):
    depth = w_in.shape[0]
    n_prompt, t_prompt, _ = x_prompt.shape
    n_sample, t_sample, _ = x_sample.shape
    xp, xs = x_prompt, x_sample
    outs = [[] for _ in range(8)]
    for l in range(depth):
        w_lo = jnp.pad(w_in[l][:, MAIN_WIDTH:], ((0, 0), (0, LANES - GATE_RANK)))
        w = {
            "g_mix_pre": g_mix_pre[l][None],
            "w_main": w_in[l][:, :MAIN_WIDTH].astype(BF16),
            "w_lo": w_lo.astype(BF16),
            "w_gate": jnp.pad(w_gate_up[l], ((0, LANES - GATE_RANK), (0, 0))).astype(BF16),
            "b_gate": b_gate[l][None],
            "g_gla": g_gla[l][None],
            "w_o": w_o[l].astype(BF16),
            "g_mix_post": g_mix_post[l][None],
            "g_ffn_pre": g_ffn_pre[l][None],
            "w_up": w_up[l].astype(BF16),
            "w_conv": w_conv[l],
            "b_conv": b_conv[l][None],
            "w_down": w_down[l].astype(BF16),
            "g_ffn_post": g_ffn_post[l][None],
        }
        keep = min(BAND_PAST, t_prompt)
        assert keep <= PROMPT_ROW_TILE and t_prompt % PROMPT_ROW_TILE == 0
        g_prompt = _bias_diagonals(rel_bias[l], ATT_TQ, ATT_GPAD)
        zero_prev = jnp.zeros((n_prompt, 1, D_FF), F32)
        xp, kp, vp, gp, cp = _layer(
            xp, n_prompt, t_prompt,
            lambda proj: _attn_prompt(proj, g_prompt, n_prompt, t_prompt),
            jnp.zeros((n_prompt, N_HEADS_GLA, DK_GLA, DV_GLA), F32),
            zero_prev, zero_prev, 1, w, PROMPT_ROW_TILE, PROMPT_MID_TILE, keep, t_prompt // PROMPT_ROW_TILE)
        for lst, val in zip(outs[:4], (kp, vp, gp, cp)):
            lst.append(val)
        g_sample = _bias_diagonals(rel_bias[l], t_sample, SAMPLE_GPAD)
        m_sample = n_sample * t_sample
        prev0 = jnp.repeat(state_conv[l][:, 0], t_sample, axis=0)[None]
        prev1 = jnp.repeat(state_conv[l][:, 1], t_sample, axis=0)[None]
        xs, kn, vn, gn, cn = _layer(
            xs, n_sample, t_sample,
            lambda proj: _attn_sample(proj, cache_k[l], cache_v[l], g_sample, n_sample, t_sample),
            state_gla[l], prev0, prev1, n_sample, w, m_sample, m_sample, m_sample, 1)
        for lst, val in zip(outs[4:], (kn, vn, gn, cn)):
            lst.append(val)
    return (xp, xs) + tuple(jnp.stack(o) for o in outs)
```

```python
import functools

import jax
import jax.numpy as jnp
from jax import lax
from jax.experimental import pallas as pl
from jax.experimental.pallas import tpu as pltpu

D_MODEL = 2048
CHUNK = 64
N_PAST_CHUNKS = 8
BAND_PAST = CHUNK * N_PAST_CHUNKS
ATT_WIDTH = 1024
HEAD_DIM_ATT = 128
N_HEADS_ATT = 8
MAX_REL = 128
GLA_V_WIDTH = 1024
N_HEADS_GLA = 4
DV_GLA = 256
DK_GLA = 128
GLA_QK_WIDTH = 512
GATE_RANK = 16
GATE_TAU = 16.0
D_FF = 5632
CONV_W = 3
EPS = 1e-6
MAIN_WIDTH = 3 * ATT_WIDTH + 2 * GLA_QK_WIDTH + 2 * GLA_V_WIDTH

LANES = 128
VMEM_LIMIT_BYTES = 56 * 1024 * 1024
NEG_INF = -1e30

F32 = jnp.float32
BF16 = jnp.bfloat16

NT_DIMS = (((1,), (1,)), ((), ()))
TN_DIMS = (((0,), (0,)), ((), ()))


def _rms(x, g):
    return x * lax.rsqrt(jnp.mean(x * x, axis=-1, keepdims=True) + EPS) * g


def _params(*sem):
    return pltpu.CompilerParams(dimension_semantics=sem, vmem_limit_bytes=VMEM_LIMIT_BYTES)


IN_TN = 1024


def _inproj_kernel(x_ref, g_ref, w_ref, wlo_ref, proj_ref, k_ref, v_ref, alo_ref, h_ref):
    j = pl.program_id(1)

    @pl.when(j == 0)
    def _():
        h = _rms(x_ref[...], g_ref[...]).astype(BF16)
        h_ref[...] = h
        alo_ref[...] = jnp.dot(h, wlo_ref[...], preferred_element_type=F32).astype(alo_ref.dtype)

    acc = jnp.dot(h_ref[...], w_ref[...], preferred_element_type=F32)
    proj_ref[...] = acc.astype(proj_ref.dtype)

    @pl.when(j == 1)
    def _():
        k_ref[...] = acc[acc.shape[0] - k_ref.shape[0]:, :]

    @pl.when(j == 2)
    def _():
        v_ref[...] = acc[acc.shape[0] - v_ref.shape[0]:, :]


def _in_proj(x, g, w_main, w_lo, tm, kv_rows, kv_group):
    m = x.shape[0]
    n_tiles = MAIN_WIDTH // IN_TN
    kv_spec = pl.BlockSpec((kv_rows, ATT_WIDTH), lambda i, j: (i // kv_group, 0))
    kv_shape = jax.ShapeDtypeStruct((m // (tm * kv_group) * kv_rows, ATT_WIDTH), F32)
    return pl.pallas_call(
        _inproj_kernel,
        grid=(m // tm, n_tiles),
        in_specs=[
            pl.BlockSpec((tm, D_MODEL), lambda i, j: (i, 0)),
            pl.BlockSpec((1, D_MODEL), lambda i, j: (0, 0)),
            pl.BlockSpec((D_MODEL, IN_TN), lambda i, j: (0, j)),
            pl.BlockSpec((D_MODEL, LANES), lambda i, j: (0, 0)),
        ],
        out_specs=[
            pl.BlockSpec((tm, IN_TN), lambda i, j: (i, j)),
            kv_spec,
            kv_spec,
            pl.BlockSpec((tm, LANES), lambda i, j: (i, 0)),
        ],
        out_shape=[
            jax.ShapeDtypeStruct((m, MAIN_WIDTH), BF16),
            kv_shape,
            kv_shape,
            jax.ShapeDtypeStruct((m, LANES), BF16),
        ],
        scratch_shapes=[pltpu.VMEM((tm, D_MODEL), BF16)],
        compiler_params=_params("arbitrary", "arbitrary"),
        name="in_proj",
    )(x, g, w_main, w_lo)


ATT_TQ = 2 * CHUNK
ATT_KWIN = BAND_PAST + ATT_TQ
ATT_GPAD = 768
ATT_UNROLL = 4


def _toeplitz(g_row, tq, width):
    gpad = g_row.shape[-1]
    rolled = pltpu.roll(jnp.broadcast_to(g_row, (tq, gpad)), gpad - (tq - 1), axis=1, stride=1, stride_axis=0)
    return rolled[:, :width]


def _attn_prompt_kernel(q_ref, k_ref, v_ref, g_ref, o_ref, kp_ref, vp_ref, bias_ref):
    t_len = q_ref.shape[0]
    zeros = jnp.zeros((BAND_PAST, HEAD_DIM_ATT), BF16)
    kp_ref[0:BAND_PAST, :] = zeros
    vp_ref[0:BAND_PAST, :] = zeros
    kp_ref[BAND_PAST:, :] = k_ref[...]
    vp_ref[BAND_PAST:, :] = v_ref[...]

    qi = lax.broadcasted_iota(jnp.int32, (ATT_TQ, ATT_KWIN), 0)
    kj = lax.broadcasted_iota(jnp.int32, (ATT_TQ, ATT_KWIN), 1)
    qc = qi // CHUNK
    kc = kj // CHUNK
    in_band = (kc >= qc) & (kc <= qc + N_PAST_CHUNKS)
    bias_ref[...] = jnp.where(in_band, _toeplitz(g_ref[0], ATT_TQ, ATT_KWIN), NEG_INF)

    scale = HEAD_DIM_ATT ** -0.5

    def body(it, carry, *, has_padding):
        starts = [pl.multiple_of((it * ATT_UNROLL + u) * ATT_TQ, ATT_TQ) for u in range(ATT_UNROLL)]
        scores = [lax.dot_general(q_ref[pl.ds(r0, ATT_TQ), :], kp_ref[pl.ds(r0, ATT_KWIN), :], NT_DIMS,
                                  preferred_element_type=F32) for r0 in starts]
        probs, sums = [], []
        for r0, s in zip(starts, scores):
            s = s * scale + bias_ref[...]
            if has_padding:
                s = jnp.where(kj + (r0 - BAND_PAST) >= 0, s, NEG_INF)
            p = jnp.exp(s - jnp.max(s, axis=-1, keepdims=True))
            sums.append(jnp.sum(p, axis=-1, keepdims=True))
            probs.append(p.astype(BF16))
        for r0, p, l in zip(starts, probs, sums):
            o = jnp.dot(p, vp_ref[pl.ds(r0, ATT_KWIN), :], preferred_element_type=F32) / l
            o_ref[pl.ds(r0, ATT_TQ), :] = o.astype(o_ref.dtype)
        return carry

    n_iters = t_len // (ATT_TQ * ATT_UNROLL)
    n_padded = min(BAND_PAST // (ATT_TQ * ATT_UNROLL), n_iters)
    lax.fori_loop(0, n_padded, functools.partial(body, has_padding=True), 0)
    lax.fori_loop(n_padded, n_iters, functools.partial(body, has_padding=False), 0)


def _attn_prompt(proj, g_tab, n_batch, t_len):
    m = proj.shape[0]
    blk = (t_len, HEAD_DIM_ATT)
    return pl.pallas_call(
        _attn_prompt_kernel,
        grid=(n_batch, N_HEADS_ATT),
        in_specs=[
            pl.BlockSpec(blk, lambda b, h: (b, h)),
            pl.BlockSpec(blk, lambda b, h: (b, N_HEADS_ATT + h)),
            pl.BlockSpec(blk, lambda b, h: (b, 2 * N_HEADS_ATT + h)),
            pl.BlockSpec((1, 1, ATT_GPAD), lambda b, h: (h, 0, 0)),
        ],
        out_specs=pl.BlockSpec(blk, lambda b, h: (b, h)),
        out_shape=jax.ShapeDtypeStruct((m, ATT_WIDTH), BF16),
        scratch_shapes=[
            pltpu.VMEM((BAND_PAST + t_len, HEAD_DIM_ATT), BF16),
            pltpu.VMEM((BAND_PAST + t_len, HEAD_DIM_ATT), BF16),
            pltpu.VMEM((ATT_TQ, ATT_KWIN), F32),
        ],
        compiler_params=_params("parallel", "arbitrary"),
        name="attn_prompt",
    )(proj, proj, proj, g_tab)


SAMPLE_GPAD = 640


def _attn_sample_kernel(q_ref, kn_ref, vn_ref, ck_ref, cv_ref, g_ref, o_ref):
    tq = q_ref.shape[0]
    n_cache = ck_ref.shape[1]
    scale = HEAD_DIM_ATT ** -0.5
    heads = range(N_HEADS_ATT)
    sls = [slice(h * HEAD_DIM_ATT, (h + 1) * HEAD_DIM_ATT) for h in heads]
    ck = jnp.swapaxes(ck_ref[0], 0, 1).astype(BF16)
    cv = jnp.swapaxes(cv_ref[0], 0, 1).astype(BF16)
    s_cache = [lax.dot_general(q_ref[:, sls[h]], ck[h], NT_DIMS, preferred_element_type=F32) for h in heads]
    s_new = [lax.dot_general(q_ref[:, sls[h]], kn_ref[:, sls[h]], NT_DIMS, preferred_element_type=F32)
             for h in heads]
    probs = []
    for h in heads:
        bias = _toeplitz(g_ref[h], tq, n_cache + LANES)
        s1 = s_cache[h] * scale + bias[:, :n_cache]
        s2 = s_new[h] * scale + bias[:, n_cache:n_cache + tq]
        m = jnp.maximum(jnp.max(s1, axis=-1, keepdims=True), jnp.max(s2, axis=-1, keepdims=True))
        p1 = jnp.exp(s1 - m)
        p2 = jnp.exp(s2 - m)
        l = jnp.sum(p1, axis=-1, keepdims=True) + jnp.sum(p2, axis=-1, keepdims=True)
        probs.append((p1.astype(BF16), p2.astype(BF16), l))
    for h, (p1, p2, l) in zip(heads, probs):
        o = jnp.dot(p1, cv[h], preferred_element_type=F32)
        o = o + jnp.dot(p2, vn_ref[:, sls[h]], preferred_element_type=F32)
        o_ref[:, sls[h]] = (o / l).astype(o_ref.dtype)


def _attn_sample(proj, cache_k, cache_v, g_tab, n_streams, t_len):
    m = proj.shape[0]
    n_cache = cache_k.shape[1]
    blk = (t_len, ATT_WIDTH)
    return pl.pallas_call(
        _attn_sample_kernel,
        grid=(n_streams,),
        in_specs=[
            pl.BlockSpec(blk, lambda s: (s, 0)),
            pl.BlockSpec(blk, lambda s: (s, 1)),
            pl.BlockSpec(blk, lambda s: (s, 2)),
            pl.BlockSpec((1, n_cache, N_HEADS_ATT, HEAD_DIM_ATT), lambda s: (s, 0, 0, 0)),
            pl.BlockSpec((1, n_cache, N_HEADS_ATT, HEAD_DIM_ATT), lambda s: (s, 0, 0, 0)),
            pl.BlockSpec((N_HEADS_ATT, 1, SAMPLE_GPAD), lambda s: (0, 0, 0)),
        ],
        out_specs=pl.BlockSpec(blk, lambda s: (s, 0)),
        out_shape=jax.ShapeDtypeStruct((m, ATT_WIDTH), BF16),
        compiler_params=_params("parallel"),
        name="attn_sample",
    )(proj, proj, proj, cache_k, cache_v, g_tab)


GLA_UNROLL = 4


def _gla_kernel(q_ref, k_ref, v_ref, r_ref, alo_ref, wg_ref, bg_ref, gg_ref, s0_ref,
                o_ref, s_out_ref, loga_ref, st_ref, *, chunk):
    t_len = q_ref.shape[0]
    x = jnp.dot(alo_ref[...], wg_ref[...], preferred_element_type=F32) + bg_ref[...]
    loga_ref[...] = (jnp.minimum(x, 0.0) - jnp.log1p(jnp.exp(-jnp.abs(x)))) * (1.0 / GATE_TAU)

    for h in range(N_HEADS_GLA):
        st_ref[h] = s0_ref[0, h].T

    rows = lax.broadcasted_iota(jnp.int32, (chunk, GLA_QK_WIDTH), 0)
    tril = (lax.broadcasted_iota(jnp.int32, (chunk, chunk), 0)
            >= lax.broadcasted_iota(jnp.int32, (chunk, chunk), 1))
    mid_row = chunk // 2 - 1

    heads = range(N_HEADS_GLA)
    ksl = [slice(h * DK_GLA, (h + 1) * DK_GLA) for h in heads]
    vsl = [slice(h * DV_GLA, (h + 1) * DV_GLA) for h in heads]
    n_unroll = min(GLA_UNROLL, t_len // chunk)

    def body(it, carry):
        starts = [pl.multiple_of((it * n_unroll + u) * chunk, chunk) for u in range(n_unroll)]
        scores, kv_sums, q_ins, decays = [], [], [], []
        for r0 in starts:
            c = loga_ref[pl.ds(r0, chunk), :]
            step = 1
            while step < chunk:
                c = c + jnp.where(rows >= step, pltpu.roll(c, step, axis=0), 0.0)
                step *= 2
            mid = c[mid_row:mid_row + 1, :]
            last = c[chunk - 1:chunk, :]
            q = q_ref[pl.ds(r0, chunk), :].astype(F32) * (DK_GLA ** -0.5)
            k = k_ref[pl.ds(r0, chunk), :].astype(F32)
            q_mid = (q * jnp.exp(c - mid)).astype(BF16)
            k_mid = (k * jnp.exp(mid - c)).astype(BF16)
            k_out = (k * jnp.exp(last - c)).astype(BF16)
            q_ins.append((q * jnp.exp(c)).astype(BF16))
            decays.append(jnp.exp(last))
            scores.append([lax.dot_general(q_mid[:, ksl[h]], k_mid[:, ksl[h]], NT_DIMS, preferred_element_type=F32)
                           for h in heads])
            kv_sums.append([lax.dot_general(v_ref[pl.ds(r0, chunk), vsl[h]], k_out[:, ksl[h]], TN_DIMS,
                                            preferred_element_type=F32) for h in heads])
        states = [st_ref[h] for h in heads]
        for u, r0 in enumerate(starts):
            inter = [lax.dot_general(q_ins[u][:, ksl[h]], states[h].astype(BF16), NT_DIMS,
                                     preferred_element_type=F32) for h in heads]
            states = [states[h] * decays[u][:, ksl[h]] + kv_sums[u][h] for h in heads]
            for h in heads:
                a = jnp.where(tril, scores[u][h], 0.0).astype(BF16)
                o = jnp.dot(a, v_ref[pl.ds(r0, chunk), vsl[h]], preferred_element_type=F32) + inter[h]
                r = r_ref[pl.ds(r0, chunk), vsl[h]].astype(F32)
                o_ref[pl.ds(r0, chunk), vsl[h]] = (_rms(o, gg_ref[...]) * (r * jax.nn.sigmoid(r))).astype(o_ref.dtype)
        for h in heads:
            st_ref[h] = states[h]
        return carry

    lax.fori_loop(0, t_len // (chunk * n_unroll), body, 0)

    for h in range(N_HEADS_GLA):
        s_out_ref[0, h] = st_ref[h].T


def _gla(proj, a_lo, w_gate, b_gate, g_gla, state0, n_streams, t_len):
    m = proj.shape[0]
    chunk = min(32, t_len)
    qk_blk = (t_len, GLA_QK_WIDTH)
    v_blk = (t_len, GLA_V_WIDTH)
    qk0 = 3 * ATT_WIDTH // GLA_QK_WIDTH
    v0 = (3 * ATT_WIDTH + 2 * GLA_QK_WIDTH) // GLA_V_WIDTH
    state_blk = (1, N_HEADS_GLA, DK_GLA, DV_GLA)
    return pl.pallas_call(
        functools.partial(_gla_kernel, chunk=chunk),
        grid=(n_streams,),
        in_specs=[
            pl.BlockSpec(qk_blk, lambda s: (s, qk0)),
            pl.BlockSpec(qk_blk, lambda s: (s, qk0 + 1)),
            pl.BlockSpec(v_blk, lambda s: (s, v0)),
            pl.BlockSpec(v_blk, lambda s: (s, v0 + 1)),
            pl.BlockSpec((t_len, LANES), lambda s: (s, 0)),
            pl.BlockSpec((LANES, GLA_QK_WIDTH), lambda s: (0, 0)),
            pl.BlockSpec((1, GLA_QK_WIDTH), lambda s: (0, 0)),
            pl.BlockSpec((1, DV_GLA), lambda s: (0, 0)),
            pl.BlockSpec(state_blk, lambda s: (s, 0, 0, 0)),
        ],
        out_specs=[
            pl.BlockSpec(v_blk, lambda s: (s, 0)),
            pl.BlockSpec(state_blk, lambda s: (s, 0, 0, 0)),
        ],
        out_shape=[
            jax.ShapeDtypeStruct((m, GLA_V_WIDTH), BF16),
            jax.ShapeDtypeStruct((n_streams, N_HEADS_GLA, DK_GLA, DV_GLA), F32),
        ],
        scratch_shapes=[
            pltpu.VMEM((t_len, GLA_QK_WIDTH), F32),
            pltpu.VMEM((N_HEADS_GLA, DV_GLA, DK_GLA), F32),
        ],
        compiler_params=_params("parallel"),
        name="gla",
    )(proj, proj, proj, proj, a_lo, w_gate, b_gate, g_gla, state0)


def _outproj_kernel(oa_ref, og_ref, wa_ref, wg_ref, x_ref, gpost_ref, gpre_ref, x1_ref, h2_ref):
    mix = jnp.dot(oa_ref[...], wa_ref[...], preferred_element_type=F32)
    mix = mix + jnp.dot(og_ref[...], wg_ref[...], preferred_element_type=F32)
    x1 = x_ref[...] + _rms(mix, gpost_ref[...])
    x1_ref[...] = x1
    h2_ref[...] = _rms(x1, gpre_ref[...]).astype(h2_ref.dtype)


def _out_proj(o_att, o_gla, w_o, x, g_post, g_pre, tm):
    m = x.shape[0]
    row = lambda i: (i, 0)
    const = lambda i: (0, 0)
    return pl.pallas_call(
        _outproj_kernel,
        grid=(m // tm,),
        in_specs=[
            pl.BlockSpec((tm, ATT_WIDTH), row),
            pl.BlockSpec((tm, GLA_V_WIDTH), row),
            pl.BlockSpec((ATT_WIDTH, D_MODEL), const),
            pl.BlockSpec((GLA_V_WIDTH, D_MODEL), lambda i: (1, 0)),
            pl.BlockSpec((tm, D_MODEL), row),
            pl.BlockSpec((1, D_MODEL), const),
            pl.BlockSpec((1, D_MODEL), const),
        ],
        out_specs=[pl.BlockSpec((tm, D_MODEL), row), pl.BlockSpec((tm, D_MODEL), row)],
        out_shape=[jax.ShapeDtypeStruct((m, D_MODEL), F32), jax.ShapeDtypeStruct((m, D_MODEL), BF16)],
        compiler_params=_params("parallel"),
        name="out_proj",
    )(o_att, o_gla, w_o, w_o, x, g_post, g_pre)


FFN_TN = 512


FFN_SUB = 256


def _ffn_up_kernel(h_ref, wg_ref, wv_ref, p0_ref, p1_ref, wc_ref, bc_ref, act_ref, cs_ref, *, t_len):
    tm, tn = act_ref.shape
    h = h_ref[...]
    pieces = [slice(c, c + FFN_SUB) for c in range(0, tn, FFN_SUB)]
    prods = [(jnp.dot(h, wg_ref[:, cs], preferred_element_type=F32),
              jnp.dot(h, wv_ref[:, cs], preferred_element_type=F32)) for cs in pieces]
    fix_rows = 16 if tm == t_len else tm
    pos = lax.broadcasted_iota(jnp.int32, (fix_rows, FFN_SUB), 0) & (t_len - 1)
    for cs, (gate, val) in zip(pieces, prods):
        w0, w1, w2, bias = wc_ref[0:1, cs], wc_ref[1:2, cs], wc_ref[2:3, cs], bc_ref[:, cs]
        back1 = pltpu.roll(gate, 1, axis=0)
        back2 = pltpu.roll(gate, 2, axis=0)
        p0 = p0_ref[0, :, cs]
        p1 = p1_ref[0, :, cs]
        top1 = jnp.where(pos == 0, p1, back1[:fix_rows])
        top2 = jnp.where(pos == 0, p0, jnp.where(pos == 1, p1, back2[:fix_rows]))
        top = bias + (top2 * w0 + top1 * w1 + gate[:fix_rows] * w2)
        if fix_rows < tm:
            conv = bias + (back2 * w0 + back1 * w1 + gate * w2)
            act_ref[:, cs] = (conv * jax.nn.sigmoid(conv) * val).astype(act_ref.dtype)
        act_ref[:fix_rows, cs] = (top * jax.nn.sigmoid(top) * val[:fix_rows]).astype(act_ref.dtype)
        cs_ref[:, :, cs] = gate.reshape(tm // t_len, t_len, FFN_SUB)[:, t_len - (CONV_W - 1):, :]


def _ffn_up(h2, w_up, prev0, prev1, w_conv, b_conv, t_len, streams_per_tile):
    m = h2.shape[0]
    tm = t_len * streams_per_tile
    n_streams = m // t_len
    n_tiles = D_FF // FFN_TN
    prev_rows = prev0.shape[1]
    return pl.pallas_call(
        functools.partial(_ffn_up_kernel, t_len=t_len),
        grid=(m // tm, n_tiles),
        in_specs=[
            pl.BlockSpec((tm, D_MODEL), lambda i, j: (i, 0)),
            pl.BlockSpec((D_MODEL, FFN_TN), lambda i, j: (0, j)),
            pl.BlockSpec((D_MODEL, FFN_TN), lambda i, j: (0, n_tiles + j)),
            pl.BlockSpec((1, prev_rows, FFN_TN), lambda i, j: (i, 0, j)),
            pl.BlockSpec((1, prev_rows, FFN_TN), lambda i, j: (i, 0, j)),
            pl.BlockSpec((CONV_W, FFN_TN), lambda i, j: (0, j)),
            pl.BlockSpec((1, FFN_TN), lambda i, j: (0, j)),
        ],
        out_specs=[
            pl.BlockSpec((tm, FFN_TN), lambda i, j: (i, j)),
            pl.BlockSpec((streams_per_tile, CONV_W - 1, FFN_TN), lambda i, j: (i, 0, j)),
        ],
        out_shape=[
            jax.ShapeDtypeStruct((m, D_FF), BF16),
            jax.ShapeDtypeStruct((n_streams, CONV_W - 1, D_FF), F32),
        ],
        compiler_params=_params("parallel", "arbitrary"),
        name="ffn_up",
    )(h2, w_up, w_up, prev0, prev1, w_conv, b_conv)


DOWN_TK = 512


def _ffn_down_kernel(a_ref, w_ref, x1_ref, g_ref, o_ref, acc_ref):
    kk = pl.program_id(1)

    @pl.when(kk == 0)
    def _():
        acc_ref[...] = jnp.zeros_like(acc_ref)

    acc_ref[...] += jnp.dot(a_ref[...], w_ref[...], preferred_element_type=F32)

    @pl.when(kk == pl.num_programs(1) - 1)
    def _():
        o_ref[...] = x1_ref[...] + _rms(acc_ref[...], g_ref[...])


def _ffn_down(act, w_down, x1, g_post, tm):
    m = act.shape[0]
    return pl.pallas_call(
        _ffn_down_kernel,
        grid=(m // tm, D_FF // DOWN_TK),
        in_specs=[
            pl.BlockSpec((tm, DOWN_TK), lambda i, k: (i, k)),
            pl.BlockSpec((DOWN_TK, D_MODEL), lambda i, k: (k, 0)),
            pl.BlockSpec((tm, D_MODEL), lambda i, k: (i, 0)),
            pl.BlockSpec((1, D_MODEL), lambda i, k: (0, 0)),
        ],
        out_specs=pl.BlockSpec((tm, D_MODEL), lambda i, k: (i, 0)),
        out_shape=jax.ShapeDtypeStruct((m, D_MODEL), F32),
        scratch_shapes=[pltpu.VMEM((tm, D_MODEL), F32)],
        compiler_params=_params("parallel", "arbitrary"),
        name="ffn_down",
    )(act, w_down, x1, g_post)


PROMPT_ROW_TILE = 1024
PROMPT_MID_TILE = 512


def _bias_diagonals(rel_bias, tq, width):
    rel = BAND_PAST + (tq - 1) - jnp.arange(width)
    idx = jnp.clip(rel, -MAX_REL, MAX_REL) + MAX_REL
    return rel_bias[:, idx].reshape(N_HEADS_ATT, 1, width)


def _layer(x, n_streams, t_len, att_fn, gla_state0, prev0, prev1, streams_per_tile, w, row_tile, mid_tile,
           kv_rows, kv_group):
    m = n_streams * t_len
    x2 = x.reshape(m, D_MODEL)
    proj, k32, v32, a_lo = _in_proj(x2, w["g_mix_pre"], w["w_main"], w["w_lo"], row_tile, kv_rows, kv_group)
    o_att = att_fn(proj)
    o_gla, s_gla = _gla(proj, a_lo, w["w_gate"], w["b_gate"], w["g_gla"], gla_state0, n_streams, t_len)
    x1, h2 = _out_proj(o_att, o_gla, w["w_o"], x2, w["g_mix_post"], w["g_ffn_pre"], mid_tile)
    act, conv_state = _ffn_up(h2, w["w_up"], prev0, prev1, w["w_conv"], w["b_conv"], t_len, streams_per_tile)
    y = _ffn_down(act, w["w_down"], x1, w["g_ffn_post"], row_tile)
    kv_shape = (n_streams, k32.shape[0] // n_streams, N_HEADS_ATT, HEAD_DIM_ATT)
    return y.reshape(n_streams, t_len, D_MODEL), k32.reshape(kv_shape), v32.reshape(kv_shape), s_gla, conv_state


def kernel(x_prompt, x_sample, cache_k, cache_v, state_gla, state_conv, g_mix_pre, w_in, w_gate_up, b_gate,
           rel_bias, g_gla, w_o, g_mix_post, g_ffn_pre, w_up, w_conv, b_conv, w_down, g_ffn_post):
    depth = w_in.shape[0]
    n_prompt, t_prompt, _ = x_prompt.shape
    n_sample, t_sample, _ = x_sample.shape
    xp, xs = x_prompt, x_sample
    outs = [[] for _ in range(8)]
    for l in range(depth):
        w_lo = jnp.pad(w_in[l][:, MAIN_WIDTH:], ((0, 0), (0, LANES - GATE_RANK)))
        w = {
            "g_mix_pre": g_mix_pre[l][None],
            "w_main": w_in[l].astype(BF16),
            "w_lo": w_lo.astype(BF16),
            "w_gate": jnp.pad(w_gate_up[l], ((0, LANES - GATE_RANK), (0, 0))).astype(BF16),
            "b_gate": b_gate[l][None],
            "g_gla": g_gla[l][None],
            "w_o": w_o[l].astype(BF16),
            "g_mix_post": g_mix_post[l][None],
            "g_ffn_pre": g_ffn_pre[l][None],
            "w_up": w_up[l].astype(BF16),
            "w_conv": w_conv[l],
            "b_conv": b_conv[l][None],
            "w_down": w_down[l].astype(BF16),
            "g_ffn_post": g_ffn_post[l][None],
        }
        keep = min(BAND_PAST, t_prompt)
        assert keep <= PROMPT_ROW_TILE and t_prompt % PROMPT_ROW_TILE == 0
        g_prompt = _bias_diagonals(rel_bias[l], ATT_TQ, ATT_GPAD)
        zero_prev = jnp.zeros((n_prompt, 1, D_FF), F32)
        xp, kp, vp, gp, cp = _layer(
            xp, n_prompt, t_prompt,
            lambda proj: _attn_prompt(proj, g_prompt, n_prompt, t_prompt),
            jnp.zeros((n_prompt, N_HEADS_GLA, DK_GLA, DV_GLA), F32),
            zero_prev, zero_prev, 1, w, PROMPT_ROW_TILE, PROMPT_MID_TILE, keep, t_prompt // PROMPT_ROW_TILE)
        for lst, val in zip(outs[:4], (kp, vp, gp, cp)):
            lst.append(val)
        g_sample = _bias_diagonals(rel_bias[l], t_sample, SAMPLE_GPAD)
        m_sample = n_sample * t_sample
        prev0 = jnp.repeat(state_conv[l][:, 0], t_sample, axis=0)[None]
        prev1 = jnp.repeat(state_conv[l][:, 1], t_sample, axis=0)[None]
        xs, kn, vn, gn, cn = _layer(
            xs, n_sample, t_sample,
            lambda proj: _attn_sample(proj, cache_k[l], cache_v[l], g_sample, n_sample, t_sample),
            state_gla[l], prev0, prev1, n_sample, w, m_sample, m_sample, m_sample, 1)
        for lst, val in zip(outs[4:], (kn, vn, gn, cn)):
            lst.append(val)
    return (xp, xs) + tuple(jnp.stack(o) for o in outs)
```

```python
import functools

import jax
import jax.numpy as jnp
from jax import lax
from jax.experimental import pallas as pl
from jax.experimental.pallas import tpu as pltpu

D_MODEL = 2048
CHUNK = 64
N_PAST_CHUNKS = 8
BAND_PAST = CHUNK * N_PAST_CHUNKS
ATT_WIDTH = 1024
HEAD_DIM_ATT = 128
N_HEADS_ATT = 8
MAX_REL = 128
GLA_V_WIDTH = 1024
N_HEADS_GLA = 4
DV_GLA = 256
DK_GLA = 128
GLA_QK_WIDTH = 512
GATE_RANK = 16
GATE_TAU = 16.0
D_FF = 5632
CONV_W = 3
EPS = 1e-6
MAIN_WIDTH = 3 * ATT_WIDTH + 2 * GLA_QK_WIDTH + 2 * GLA_V_WIDTH

LANES = 128
VMEM_LIMIT_BYTES = 56 * 1024 * 1024
NEG_INF = -1e30

F32 = jnp.float32
BF16 = jnp.bfloat16

NT_DIMS = (((1,), (1,)), ((), ()))
TN_DIMS = (((0,), (0,)), ((), ()))


def _rms(x, g):
    return x * lax.rsqrt(jnp.mean(x * x, axis=-1, keepdims=True) + EPS) * g


def _params(*sem):
    return pltpu.CompilerParams(dimension_semantics=sem, vmem_limit_bytes=VMEM_LIMIT_BYTES)


IN_TN = 1024


ATT_TILES = 3 * ATT_WIDTH // IN_TN
HEADS_PER_TILE = IN_TN // HEAD_DIM_ATT


def _inproj_kernel(x_ref, g_ref, w_ref, wlo_ref, att_ref, gla_ref, k_ref, v_ref, alo_ref, h_ref):
    j = pl.program_id(1)

    @pl.when(j == 0)
    def _():
        h = _rms(x_ref[...], g_ref[...]).astype(BF16)
        h_ref[...] = h
        alo_ref[...] = jnp.dot(h, wlo_ref[...], preferred_element_type=F32).astype(alo_ref.dtype)

    acc = jnp.dot(h_ref[...], w_ref[...], preferred_element_type=F32)

    @pl.when(j < ATT_TILES)
    def _():
        for hh in range(HEADS_PER_TILE):
            att_ref[hh] = acc[:, hh * HEAD_DIM_ATT:(hh + 1) * HEAD_DIM_ATT].astype(att_ref.dtype)

    @pl.when(j >= ATT_TILES)
    def _():
        gla_ref[...] = acc.astype(gla_ref.dtype)

    @pl.when(j == 1)
    def _():
        k_ref[...] = acc[acc.shape[0] - k_ref.shape[0]:, :]

    @pl.when(j == 2)
    def _():
        v_ref[...] = acc[acc.shape[0] - v_ref.shape[0]:, :]


def _in_proj(x, g, w_main, w_lo, tm, kv_rows, kv_group):
    m = x.shape[0]
    n_tiles = MAIN_WIDTH // IN_TN
    kv_spec = pl.BlockSpec((kv_rows, ATT_WIDTH), lambda i, j: (i // kv_group, 0))
    kv_shape = jax.ShapeDtypeStruct((m // (tm * kv_group) * kv_rows, ATT_WIDTH), F32)
    return pl.pallas_call(
        _inproj_kernel,
        grid=(m // tm, n_tiles),
        in_specs=[
            pl.BlockSpec((tm, D_MODEL), lambda i, j: (i, 0)),
            pl.BlockSpec((1, D_MODEL), lambda i, j: (0, 0)),
            pl.BlockSpec((D_MODEL, IN_TN), lambda i, j: (0, j)),
            pl.BlockSpec((D_MODEL, LANES), lambda i, j: (0, 0)),
        ],
        out_specs=[
            pl.BlockSpec((HEADS_PER_TILE, tm, HEAD_DIM_ATT), lambda i, j: (jnp.minimum(j, ATT_TILES - 1), i, 0)),
            pl.BlockSpec((tm, IN_TN), lambda i, j: (i, jnp.maximum(j - ATT_TILES, 0))),
            kv_spec,
            kv_spec,
            pl.BlockSpec((tm, LANES), lambda i, j: (i, 0)),
        ],
        out_shape=[
            jax.ShapeDtypeStruct((3 * N_HEADS_ATT, m, HEAD_DIM_ATT), BF16),
            jax.ShapeDtypeStruct((m, MAIN_WIDTH - 3 * ATT_WIDTH), BF16),
            kv_shape,
            kv_shape,
            jax.ShapeDtypeStruct((m, LANES), BF16),
        ],
        scratch_shapes=[pltpu.VMEM((tm, D_MODEL), BF16)],
        compiler_params=_params("arbitrary", "arbitrary"),
        name="in_proj",
    )(x, g, w_main, w_lo)


ATT_TQ = 2 * CHUNK
ATT_KWIN = BAND_PAST + ATT_TQ
ATT_GPAD = 768
ATT_UNROLL = 4


def _toeplitz(g_row, tq, width):
    gpad = g_row.shape[-1]
    rolled = pltpu.roll(jnp.broadcast_to(g_row, (tq, gpad)), gpad - (tq - 1), axis=1, stride=1, stride_axis=0)
    return rolled[:, :width]


def _attn_prompt_kernel(q_ref, k_ref, v_ref, g_ref, o_ref, kp_ref, vp_ref, bias_ref):
    t_len = q_ref.shape[0]
    zeros = jnp.zeros((BAND_PAST, HEAD_DIM_ATT), BF16)
    kp_ref[0:BAND_PAST, :] = zeros
    vp_ref[0:BAND_PAST, :] = zeros
    kp_ref[BAND_PAST:, :] = k_ref[...]
    vp_ref[BAND_PAST:, :] = v_ref[...]

    qi = lax.broadcasted_iota(jnp.int32, (ATT_TQ, ATT_KWIN), 0)
    kj = lax.broadcasted_iota(jnp.int32, (ATT_TQ, ATT_KWIN), 1)
    qc = qi // CHUNK
    kc = kj // CHUNK
    in_band = (kc >= qc) & (kc <= qc + N_PAST_CHUNKS)
    bias_ref[...] = jnp.where(in_band, _toeplitz(g_ref[0], ATT_TQ, ATT_KWIN), NEG_INF)

    scale = HEAD_DIM_ATT ** -0.5

    def body(it, carry, *, has_padding):
        starts = [pl.multiple_of((it * ATT_UNROLL + u) * ATT_TQ, ATT_TQ) for u in range(ATT_UNROLL)]
        scores = [lax.dot_general(q_ref[pl.ds(r0, ATT_TQ), :], kp_ref[pl.ds(r0, ATT_KWIN), :], NT_DIMS,
                                  preferred_element_type=F32) for r0 in starts]
        probs, sums = [], []
        for r0, s in zip(starts, scores):
            s = s * scale + bias_ref[...]
            if has_padding:
                s = jnp.where(kj + (r0 - BAND_PAST) >= 0, s, NEG_INF)
            p = jnp.exp(s - jnp.max(s, axis=-1, keepdims=True))
            sums.append(jnp.sum(p, axis=-1, keepdims=True))
            probs.append(p.astype(BF16))
        for r0, p, l in zip(starts, probs, sums):
            o = jnp.dot(p, vp_ref[pl.ds(r0, ATT_KWIN), :], preferred_element_type=F32) / l
            o_ref[pl.ds(r0, ATT_TQ), :] = o.astype(o_ref.dtype)
        return carry

    n_iters = t_len // (ATT_TQ * ATT_UNROLL)
    n_padded = min(BAND_PAST // (ATT_TQ * ATT_UNROLL), n_iters)
    lax.fori_loop(0, n_padded, functools.partial(body, has_padding=True), 0)
    lax.fori_loop(n_padded, n_iters, functools.partial(body, has_padding=False), 0)


def _attn_prompt(qkv, g_tab, n_batch, t_len):
    m = qkv.shape[1]
    blk = (None, t_len, HEAD_DIM_ATT)
    return pl.pallas_call(
        _attn_prompt_kernel,
        grid=(n_batch, N_HEADS_ATT),
        in_specs=[
            pl.BlockSpec(blk, lambda b, h: (h, b, 0)),
            pl.BlockSpec(blk, lambda b, h: (N_HEADS_ATT + h, b, 0)),
            pl.BlockSpec(blk, lambda b, h: (2 * N_HEADS_ATT + h, b, 0)),
            pl.BlockSpec((1, 1, ATT_GPAD), lambda b, h: (h, 0, 0)),
        ],
        out_specs=pl.BlockSpec(blk, lambda b, h: (h, b, 0)),
        out_shape=jax.ShapeDtypeStruct((N_HEADS_ATT, m, HEAD_DIM_ATT), BF16),
        scratch_shapes=[
            pltpu.VMEM((BAND_PAST + t_len, HEAD_DIM_ATT), BF16),
            pltpu.VMEM((BAND_PAST + t_len, HEAD_DIM_ATT), BF16),
            pltpu.VMEM((ATT_TQ, ATT_KWIN), F32),
        ],
        compiler_params=_params("parallel", "arbitrary"),
        name="attn_prompt",
    )(qkv, qkv, qkv, g_tab)


SAMPLE_GPAD = 640


def _attn_sample_kernel(q_ref, kn_ref, vn_ref, ck_ref, cv_ref, g_ref, o_ref):
    tq = q_ref.shape[1]
    n_cache = ck_ref.shape[1]
    scale = HEAD_DIM_ATT ** -0.5
    heads = range(N_HEADS_ATT)
    ck = jnp.swapaxes(ck_ref[0], 0, 1).astype(BF16)
    cv = jnp.swapaxes(cv_ref[0], 0, 1).astype(BF16)
    s_cache = [lax.dot_general(q_ref[h], ck[h], NT_DIMS, preferred_element_type=F32) for h in heads]
    s_new = [lax.dot_general(q_ref[h], kn_ref[h], NT_DIMS, preferred_element_type=F32) for h in heads]
    probs = []
    for h in heads:
        bias = _toeplitz(g_ref[h], tq, n_cache + LANES)
        s1 = s_cache[h] * scale + bias[:, :n_cache]
        s2 = s_new[h] * scale + bias[:, n_cache:n_cache + tq]
        m = jnp.maximum(jnp.max(s1, axis=-1, keepdims=True), jnp.max(s2, axis=-1, keepdims=True))
        p1 = jnp.exp(s1 - m)
        p2 = jnp.exp(s2 - m)
        l = jnp.sum(p1, axis=-1, keepdims=True) + jnp.sum(p2, axis=-1, keepdims=True)
        probs.append((p1.astype(BF16), p2.astype(BF16), l))
    for h, (p1, p2, l) in zip(heads, probs):
        o = jnp.dot(p1, cv[h], preferred_element_type=F32)
        o = o + jnp.dot(p2, vn_ref[h], preferred_element_type=F32)
        o_ref[h] = (o / l).astype(o_ref.dtype)


def _attn_sample(qkv, cache_k, cache_v, g_tab, n_streams, t_len):
    m = qkv.shape[1]
    n_cache = cache_k.shape[1]
    blk = (N_HEADS_ATT, t_len, HEAD_DIM_ATT)
    return pl.pallas_call(
        _attn_sample_kernel,
        grid=(n_streams,),
        in_specs=[
            pl.BlockSpec(blk, lambda s: (0, s, 0)),
            pl.BlockSpec(blk, lambda s: (1, s, 0)),
            pl.BlockSpec(blk, lambda s: (2, s, 0)),
            pl.BlockSpec((1, n_cache, N_HEADS_ATT, HEAD_DIM_ATT), lambda s: (s, 0, 0, 0)),
            pl.BlockSpec((1, n_cache, N_HEADS_ATT, HEAD_DIM_ATT), lambda s: (s, 0, 0, 0)),
            pl.BlockSpec((N_HEADS_ATT, 1, SAMPLE_GPAD), lambda s: (0, 0, 0)),
        ],
        out_specs=pl.BlockSpec(blk, lambda s: (0, s, 0)),
        out_shape=jax.ShapeDtypeStruct((N_HEADS_ATT, m, HEAD_DIM_ATT), BF16),
        compiler_params=_params("parallel"),
        name="attn_sample",
    )(qkv, qkv, qkv, cache_k, cache_v, g_tab)


GLA_UNROLL = 4


def _gla_kernel(q_ref, k_ref, v_ref, r_ref, alo_ref, wg_ref, bg_ref, gg_ref, s0_ref,
                o_ref, s_out_ref, loga_ref, st_ref, *, chunk):
    t_len = q_ref.shape[0]
    x = jnp.dot(alo_ref[...], wg_ref[...], preferred_element_type=F32) + bg_ref[...]
    loga_ref[...] = (jnp.minimum(x, 0.0) - jnp.log1p(jnp.exp(-jnp.abs(x)))) * (1.0 / GATE_TAU)

    for h in range(N_HEADS_GLA):
        st_ref[h] = s0_ref[0, h].T

    rows = lax.broadcasted_iota(jnp.int32, (chunk, GLA_QK_WIDTH), 0)
    tril = (lax.broadcasted_iota(jnp.int32, (chunk, chunk), 0)
            >= lax.broadcasted_iota(jnp.int32, (chunk, chunk), 1))
    mid_row = chunk // 2 - 1

    heads = range(N_HEADS_GLA)
    ksl = [slice(h * DK_GLA, (h + 1) * DK_GLA) for h in heads]
    vsl = [slice(h * DV_GLA, (h + 1) * DV_GLA) for h in heads]
    n_unroll = min(GLA_UNROLL, t_len // chunk)

    def body(it, carry):
        starts = [pl.multiple_of((it * n_unroll + u) * chunk, chunk) for u in range(n_unroll)]
        scores, kv_sums, q_ins, decays = [], [], [], []
        for r0 in starts:
            c = loga_ref[pl.ds(r0, chunk), :]
            step = 1
            while step < chunk:
                c = c + jnp.where(rows >= step, pltpu.roll(c, step, axis=0), 0.0)
                step *= 2
            mid = c[mid_row:mid_row + 1, :]
            last = c[chunk - 1:chunk, :]
            q = q_ref[pl.ds(r0, chunk), :].astype(F32) * (DK_GLA ** -0.5)
            k = k_ref[pl.ds(r0, chunk), :].astype(F32)
            q_mid = (q * jnp.exp(c - mid)).astype(BF16)
            k_mid = (k * jnp.exp(mid - c)).astype(BF16)
            k_out = (k * jnp.exp(last - c)).astype(BF16)
            q_ins.append((q * jnp.exp(c)).astype(BF16))
            decays.append(jnp.exp(last))
            scores.append([lax.dot_general(q_mid[:, ksl[h]], k_mid[:, ksl[h]], NT_DIMS, preferred_element_type=F32)
                           for h in heads])
            kv_sums.append([lax.dot_general(v_ref[pl.ds(r0, chunk), vsl[h]], k_out[:, ksl[h]], TN_DIMS,
                                            preferred_element_type=F32) for h in heads])
        states = [st_ref[h] for h in heads]
        for u, r0 in enumerate(starts):
            inter = [lax.dot_general(q_ins[u][:, ksl[h]], states[h].astype(BF16), NT_DIMS,
                                     preferred_element_type=F32) for h in heads]
            states = [states[h] * decays[u][:, ksl[h]] + kv_sums[u][h] for h in heads]
            for h in heads:
                a = jnp.where(tril, scores[u][h], 0.0).astype(BF16)
                o = jnp.dot(a, v_ref[pl.ds(r0, chunk), vsl[h]], preferred_element_type=F32) + inter[h]
                r = r_ref[pl.ds(r0, chunk), vsl[h]].astype(F32)
                o_ref[pl.ds(r0, chunk), vsl[h]] = (_rms(o, gg_ref[...]) * (r * jax.nn.sigmoid(r))).astype(o_ref.dtype)
        for h in heads:
            st_ref[h] = states[h]
        return carry

    lax.fori_loop(0, t_len // (chunk * n_unroll), body, 0)

    for h in range(N_HEADS_GLA):
        s_out_ref[0, h] = st_ref[h].T


def _gla(proj, a_lo, w_gate, b_gate, g_gla, state0, n_streams, t_len):
    m = proj.shape[0]
    chunk = min(32, t_len)
    qk_blk = (t_len, GLA_QK_WIDTH)
    v_blk = (t_len, GLA_V_WIDTH)
    qk0 = 0
    v0 = 2 * GLA_QK_WIDTH // GLA_V_WIDTH
    state_blk = (1, N_HEADS_GLA, DK_GLA, DV_GLA)
    return pl.pallas_call(
        functools.partial(_gla_kernel, chunk=chunk),
        grid=(n_streams,),
        in_specs=[
            pl.BlockSpec(qk_blk, lambda s: (s, qk0)),
            pl.BlockSpec(qk_blk, lambda s: (s, qk0 + 1)),
            pl.BlockSpec(v_blk, lambda s: (s, v0)),
            pl.BlockSpec(v_blk, lambda s: (s, v0 + 1)),
            pl.BlockSpec((t_len, LANES), lambda s: (s, 0)),
            pl.BlockSpec((LANES, GLA_QK_WIDTH), lambda s: (0, 0)),
            pl.BlockSpec((1, GLA_QK_WIDTH), lambda s: (0, 0)),
            pl.BlockSpec((1, DV_GLA), lambda s: (0, 0)),
            pl.BlockSpec(state_blk, lambda s: (s, 0, 0, 0)),
        ],
        out_specs=[
            pl.BlockSpec(v_blk, lambda s: (s, 0)),
            pl.BlockSpec(state_blk, lambda s: (s, 0, 0, 0)),
        ],
        out_shape=[
            jax.ShapeDtypeStruct((m, GLA_V_WIDTH), BF16),
            jax.ShapeDtypeStruct((n_streams, N_HEADS_GLA, DK_GLA, DV_GLA), F32),
        ],
        scratch_shapes=[
            pltpu.VMEM((t_len, GLA_QK_WIDTH), F32),
            pltpu.VMEM((N_HEADS_GLA, DV_GLA, DK_GLA), F32),
        ],
        compiler_params=_params("parallel"),
        name="gla",
    )(proj, proj, proj, proj, a_lo, w_gate, b_gate, g_gla, state0)


def _outproj_kernel(oa_ref, og_ref, wa_ref, wg_ref, x_ref, gpost_ref, gpre_ref, x1_ref, h2_ref):
    o_att = jnp.concatenate([oa_ref[h] for h in range(N_HEADS_ATT)], axis=-1)
    mix = jnp.dot(o_att, wa_ref[...], preferred_element_type=F32)
    mix = mix + jnp.dot(og_ref[...], wg_ref[...], preferred_element_type=F32)
    x1 = x_ref[...] + _rms(mix, gpost_ref[...])
    x1_ref[...] = x1
    h2_ref[...] = _rms(x1, gpre_ref[...]).astype(h2_ref.dtype)


def _out_proj(o_att, o_gla, w_o, x, g_post, g_pre, tm):
    m = x.shape[0]
    row = lambda i: (i, 0)
    const = lambda i: (0, 0)
    return pl.pallas_call(
        _outproj_kernel,
        grid=(m // tm,),
        in_specs=[
            pl.BlockSpec((N_HEADS_ATT, tm, HEAD_DIM_ATT), lambda i: (0, i, 0)),
            pl.BlockSpec((tm, GLA_V_WIDTH), row),
            pl.BlockSpec((ATT_WIDTH, D_MODEL), const),
            pl.BlockSpec((GLA_V_WIDTH, D_MODEL), lambda i: (1, 0)),
            pl.BlockSpec((tm, D_MODEL), row),
            pl.BlockSpec((1, D_MODEL), const),
            pl.BlockSpec((1, D_MODEL), const),
        ],
        out_specs=[pl.BlockSpec((tm, D_MODEL), row), pl.BlockSpec((tm, D_MODEL), row)],
        out_shape=[jax.ShapeDtypeStruct((m, D_MODEL), F32), jax.ShapeDtypeStruct((m, D_MODEL), BF16)],
        compiler_params=_params("parallel"),
        name="out_proj",
    )(o_att, o_gla, w_o, w_o, x, g_post, g_pre)


FFN_TN = 512


FFN_SUB = 256


def _ffn_up_kernel(h_ref, wg_ref, wv_ref, p0_ref, p1_ref, wc_ref, bc_ref, act_ref, cs_ref, *, t_len):
    tm, tn = act_ref.shape
    h = h_ref[...]
    pieces = [slice(c, c + FFN_SUB) for c in range(0, tn, FFN_SUB)]
    prods = [(jnp.dot(h, wg_ref[:, cs], preferred_element_type=F32),
              jnp.dot(h, wv_ref[:, cs], preferred_element_type=F32)) for cs in pieces]
    fix_rows = 16 if tm == t_len else tm
    pos = lax.broadcasted_iota(jnp.int32, (fix_rows, FFN_SUB), 0) & (t_len - 1)
    for cs, (gate, val) in zip(pieces, prods):
        w0, w1, w2, bias = wc_ref[0:1, cs], wc_ref[1:2, cs], wc_ref[2:3, cs], bc_ref[:, cs]
        back1 = pltpu.roll(gate, 1, axis=0)
        back2 = pltpu.roll(gate, 2, axis=0)
        p0 = p0_ref[0, :, cs]
        p1 = p1_ref[0, :, cs]
        top1 = jnp.where(pos == 0, p1, back1[:fix_rows])
        top2 = jnp.where(pos == 0, p0, jnp.where(pos == 1, p1, back2[:fix_rows]))
        top = bias + (top2 * w0 + top1 * w1 + gate[:fix_rows] * w2)
        if fix_rows < tm:
            conv = bias + (back2 * w0 + back1 * w1 + gate * w2)
            act_ref[:, cs] = (conv * jax.nn.sigmoid(conv) * val).astype(act_ref.dtype)
        act_ref[:fix_rows, cs] = (top * jax.nn.sigmoid(top) * val[:fix_rows]).astype(act_ref.dtype)
        cs_ref[:, :, cs] = gate.reshape(tm // t_len, t_len, FFN_SUB)[:, t_len - (CONV_W - 1):, :]


def _ffn_up(h2, w_up, prev0, prev1, w_conv, b_conv, t_len, streams_per_tile):
    m = h2.shape[0]
    tm = t_len * streams_per_tile
    n_streams = m // t_len
    n_tiles = D_FF // FFN_TN
    prev_rows = prev0.shape[1]
    return pl.pallas_call(
        functools.partial(_ffn_up_kernel, t_len=t_len),
        grid=(m // tm, n_tiles),
        in_specs=[
            pl.BlockSpec((tm, D_MODEL), lambda i, j: (i, 0)),
            pl.BlockSpec((D_MODEL, FFN_TN), lambda i, j: (0, j)),
            pl.BlockSpec((D_MODEL, FFN_TN), lambda i, j: (0, n_tiles + j)),
            pl.BlockSpec((1, prev_rows, FFN_TN), lambda i, j: (i, 0, j)),
            pl.BlockSpec((1, prev_rows, FFN_TN), lambda i, j: (i, 0, j)),
            pl.BlockSpec((CONV_W, FFN_TN), lambda i, j: (0, j)),
            pl.BlockSpec((1, FFN_TN), lambda i, j: (0, j)),
        ],
        out_specs=[
            pl.BlockSpec((tm, FFN_TN), lambda i, j: (i, j)),
            pl.BlockSpec((streams_per_tile, CONV_W - 1, FFN_TN), lambda i, j: (i, 0, j)),
        ],
        out_shape=[
            jax.ShapeDtypeStruct((m, D_FF), BF16),
            jax.ShapeDtypeStruct((n_streams, CONV_W - 1, D_FF), F32),
        ],
        compiler_params=_params("parallel", "arbitrary"),
        name="ffn_up",
    )(h2, w_up, w_up, prev0, prev1, w_conv, b_conv)


DOWN_TK = 512


def _ffn_down_kernel(a_ref, w_ref, x1_ref, g_ref, o_ref, acc_ref):
    kk = pl.program_id(1)

    @pl.when(kk == 0)
    def _():
        acc_ref[...] = jnp.zeros_like(acc_ref)

    acc_ref[...] += jnp.dot(a_ref[...], w_ref[...], preferred_element_type=F32)

    @pl.when(kk == pl.num_programs(1) - 1)
    def _():
        o_ref[...] = x1_ref[...] + _rms(acc_ref[...], g_ref[...])


def _ffn_down(act, w_down, x1, g_post, tm):
    m = act.shape[0]
    return pl.pallas_call(
        _ffn_down_kernel,
        grid=(m // tm, D_FF // DOWN_TK),
        in_specs=[
            pl.BlockSpec((tm, DOWN_TK), lambda i, k: (i, k)),
            pl.BlockSpec((DOWN_TK, D_MODEL), lambda i, k: (k, 0)),
            pl.BlockSpec((tm, D_MODEL), lambda i, k: (i, 0)),
            pl.BlockSpec((1, D_MODEL), lambda i, k: (0, 0)),
        ],
        out_specs=pl.BlockSpec((tm, D_MODEL), lambda i, k: (i, 0)),
        out_shape=jax.ShapeDtypeStruct((m, D_MODEL), F32),
        scratch_shapes=[pltpu.VMEM((tm, D_MODEL), F32)],
        compiler_params=_params("parallel", "arbitrary"),
        name="ffn_down",
    )(act, w_down, x1, g_post)


PROMPT_ROW_TILE = 1024
PROMPT_MID_TILE = 512


def _bias_diagonals(rel_bias, tq, width):
    rel = BAND_PAST + (tq - 1) - jnp.arange(width)
    idx = jnp.clip(rel, -MAX_REL, MAX_REL) + MAX_REL
    return rel_bias[:, idx].reshape(N_HEADS_ATT, 1, width)


def _layer(x, n_streams, t_len, att_fn, gla_state0, prev0, prev1, streams_per_tile, w, row_tile, mid_tile,
           kv_rows, kv_group):
    m = n_streams * t_len
    x2 = x.reshape(m, D_MODEL)
    qkv, proj, k32, v32, a_lo = _in_proj(x2, w["g_mix_pre"], w["w_main"], w["w_lo"], row_tile, kv_rows, kv_group)
    o_att = att_fn(qkv)
    o_gla, s_gla = _gla(proj, a_lo, w["w_gate"], w["b_gate"], w["g_gla"], gla_state0, n_streams, t_len)
    x1, h2 = _out_proj(o_att, o_gla, w["w_o"], x2, w["g_mix_post"], w["g_ffn_pre"], mid_tile)
    act, conv_state = _ffn_up(h2, w["w_up"], prev0, prev1, w["w_conv"], w["b_conv"], t_len, streams_per_tile)
    y = _ffn_down(act, w["w_down"], x1, w["g_ffn_post"], row_tile)
    kv_shape = (n_streams, k32.shape[0] // n_streams, N_HEADS_ATT, HEAD_DIM_ATT)
    return y.reshape(n_streams, t_len, D_MODEL), k32.reshape(kv_shape), v32.reshape(kv_shape), s_gla, conv_state


def kernel(x_prompt, x_sample, cache_k, cache_v, state_gla, state_conv, g_mix_pre, w_in, w_gate_up, b_gate,
           rel_bias, g_gla, w_o, g_mix_post, g_ffn_pre, w_up, w_conv, b_conv, w_down, g_ffn_post):
    depth = w_in.shape[0]
    n_prompt, t_prompt, _ = x_prompt.shape
    n_sample, t_sample, _ = x_sample.shape
    xp, xs = x_prompt, x_sample
    outs = [[] for _ in range(8)]
    for l in range(depth):
        w_lo = jnp.pad(w_in[l][:, MAIN_WIDTH:], ((0, 0), (0, LANES - GATE_RANK)))
        w = {
            "g_mix_pre": g_mix_pre[l][None],
            "w_main": w_in[l].astype(BF16),
            "w_lo": w_lo.astype(BF16),
            "w_gate": jnp.pad(w_gate_up[l], ((0, LANES - GATE_RANK), (0, 0))).astype(BF16),
            "b_gate": b_gate[l][None],
            "g_gla": g_gla[l][None],
            "w_o": w_o[l].astype(BF16),
            "g_mix_post": g_mix_post[l][None],
            "g_ffn_pre": g_ffn_pre[l][None],
            "w_up": w_up[l].astype(BF16),
            "w_conv": w_conv[l],
            "b_conv": b_conv[l][None],
            "w_down": w_down[l].astype(BF16),
            "g_ffn_post": g_ffn_post[l][None],
        }
        keep = min(BAND_PAST, t_prompt)
        assert keep <= PROMPT_ROW_TILE and t_prompt % PROMPT_ROW_TILE == 0
        g_prompt = _bias_diagonals(rel_bias[l], ATT_TQ, ATT_GPAD)
        zero_prev = jnp.zeros((n_prompt, 1, D_FF), F32)
        xp, kp, vp, gp, cp = _layer(
            xp, n_prompt, t_prompt,
            lambda proj: _attn_prompt(proj, g_prompt, n_prompt, t_prompt),
            jnp.zeros((n_prompt, N_HEADS_GLA, DK_GLA, DV_GLA), F32),
            zero_prev, zero_prev, 1, w, PROMPT_ROW_TILE, PROMPT_MID_TILE, keep, t_prompt // PROMPT_ROW_TILE)
        for lst, val in zip(outs[:4], (kp, vp, gp, cp)):
            lst.append(val)
        g_sample = _bias_diagonals(rel_bias[l], t_sample, SAMPLE_GPAD)
        m_sample = n_sample * t_sample
        prev0 = jnp.repeat(state_conv[l][:, 0], t_sample, axis=0)[None]
        prev1 = jnp.repeat(state_conv[l][:, 1], t_sample, axis=0)[None]
        xs, kn, vn, gn, cn = _layer(
            xs, n_sample, t_sample,
            lambda proj: _attn_sample(proj, cache_k[l], cache_v[l], g_sample, n_sample, t_sample),
            state_gla[l], prev0, prev1, n_sample, w, m_sample, m_sample, m_sample, 1)
        for lst, val in zip(outs[4:], (kn, vn, gn, cn)):
            lst.append(val)
    return (xp, xs) + tuple(jnp.stack(o) for o in outs)
```
